```python
import math
import jax, jax.numpy as jnp
from jax import lax
import numpy as np

D_MODEL = 1024
BATCH = 4
SEQ = 4096
DEPTH = 1

D_CONV = D_MODEL // 2
D_SSM = D_MODEL // 2
D_MIX = D_CONV + D_SSM
CONV_WIDTH = 31
SSM_GROUP = 16
N_SSM_GROUPS = D_SSM // SSM_GROUP
SSM_STATE = 64
DT_MIN = 1e-3
DT_MAX = 1e-1
MEM_LEN = 256
N_XHEADS = 4
XHEAD_DIM = D_MODEL // N_XHEADS
N_EXPERTS = 256
TOP_K = 8
N_EXPERT_GROUPS = 8
TOPK_GROUPS = 4
D_EXPERT = D_MODEL // 4
D_SHARED = D_EXPERT
ROUTED_SCALE = 2.5
MOE_BLOCK = 128
DEEPNORM_ALPHA = (2 * DEPTH) ** 0.25
DEEPNORM_BETA = (8 * DEPTH) ** -0.25
EPS = 1e-5

kernel_name = 'hybrid_conv_s5_memxattn_moe_deepnorm'


def layer_norm(x, g, b):
    xf = x.astype(jnp.float32)
    mu = jnp.mean(xf, -1, keepdims=True)
    var = jnp.mean(jnp.square(xf - mu), -1, keepdims=True)
    return ((xf - mu) * lax.rsqrt(var + EPS) * g + b).astype(x.dtype)


def rms_norm(x, g):
    xf = x.astype(jnp.float32)
    return (xf * lax.rsqrt(jnp.mean(jnp.square(xf), -1, keepdims=True) + EPS) * g).astype(x.dtype)


def conformer_conv(h, conv_w, conv_b, ln_g, ln_b):
    y = lax.conv_general_dilated(h, conv_w[:, None, :], window_strides=(1,),
                                 padding=[(CONV_WIDTH - 1, 0)],
                                 dimension_numbers=('NWC', 'WIO', 'NWC'),
                                 feature_group_count=D_CONV) + conv_b
    return jax.nn.silu(layer_norm(y, ln_g, ln_b))


def complex_linear_combine(e1, e2):
    a1r, a1i, b1r, b1i = e1
    a2r, a2i, b2r, b2i = e2
    return (a2r * a1r - a2i * a1i,
            a2r * a1i + a2i * a1r,
            a2r * b1r - a2i * b1i + b2r,
            a2r * b1i + a2i * b1r + b2i)


def s5_ssm(u, log_dt, a_re, a_im, b_re, b_im, c_re, c_im, d_skip, w_glu, b_glu):
    f32 = jnp.float32
    bsz, seq, _ = u.shape
    uf = u.astype(f32)
    ug = uf.reshape(bsz, seq, N_SSM_GROUPS, SSM_GROUP)
    a_re = a_re.astype(f32)
    a_im = a_im.astype(f32)
    dt = jnp.exp(log_dt.astype(f32))[:, None]
    mag = jnp.exp(a_re * dt)
    ang = a_im * dt
    lb_re = mag * jnp.cos(ang)
    lb_im = mag * jnp.sin(ang)
    den = a_re * a_re + a_im * a_im
    coef_re = ((lb_re - 1.0) * a_re + lb_im * a_im) / den
    coef_im = (lb_im * a_re - (lb_re - 1.0) * a_im) / den
    b_re = b_re.astype(f32)
    b_im = b_im.astype(f32)
    bb_re = coef_re[..., None] * b_re - coef_im[..., None] * b_im
    bb_im = coef_re[..., None] * b_im + coef_im[..., None] * b_re
    bu_re = jnp.einsum('blgh,gph->blgp', ug, bb_re)
    bu_im = jnp.einsum('blgh,gph->blgp', ug, bb_im)
    lam_re = jnp.broadcast_to(lb_re, (1, seq, N_SSM_GROUPS, SSM_STATE))
    lam_im = jnp.broadcast_to(lb_im, (1, seq, N_SSM_GROUPS, SSM_STATE))
    _, _, s_re, s_im = lax.associative_scan(complex_linear_combine,
                                            (lam_re, lam_im, bu_re, bu_im), axis=1)
    y = (jnp.einsum('blgp,ghp->blgh', s_re, c_re.astype(f32))
         - jnp.einsum('blgp,ghp->blgh', s_im, c_im.astype(f32)))
    y = y.reshape(bsz, seq, D_SSM) + d_skip.astype(f32) * uf
    z = jax.nn.gelu(y)
    z = z * jax.nn.sigmoid(z @ w_glu.astype(f32) + b_glu.astype(f32))
    return z.astype(u.dtype)


def hybrid_mixer(x, w_in, conv_w, conv_b, conv_ln_g, conv_ln_b, ssm_log_dt, ssm_a_re, ssm_a_im,
                 ssm_b_re, ssm_b_im, ssm_c_re, ssm_c_im, ssm_d, w_glu, b_glu,
                 g_conv_out, g_ssm_out, w_out):
    proj = x @ w_in
    conv_val, conv_gate, ssm_u = jnp.split(proj, [D_CONV, 2 * D_CONV], axis=-1)
    conv_out = conformer_conv(conv_val * jax.nn.sigmoid(conv_gate), conv_w, conv_b,
                              conv_ln_g, conv_ln_b)
    ssm_out = s5_ssm(ssm_u, ssm_log_dt, ssm_a_re, ssm_a_im, ssm_b_re, ssm_b_im,
                     ssm_c_re, ssm_c_im, ssm_d, w_glu, b_glu)
    merged = jnp.concatenate([rms_norm(conv_out, g_conv_out), rms_norm(ssm_out, g_ssm_out)], -1)
    return merged @ w_out


def memory_cross_attention(x, mem, wq, wk, wv, wo):
    bsz, seq, d = x.shape
    q = (x @ wq).reshape(bsz, seq, N_XHEADS, XHEAD_DIM)
    k = (mem @ wk).reshape(bsz, MEM_LEN, N_XHEADS, XHEAD_DIM)
    v = (mem @ wv).reshape(bsz, MEM_LEN, N_XHEADS, XHEAD_DIM)
    s = jnp.einsum('blhd,bmhd->bhlm', q, k).astype(jnp.float32) * (XHEAD_DIM ** -0.5)
    p = jax.nn.softmax(s, axis=-1).astype(v.dtype)
    o = jnp.einsum('bhlm,bmhd->blhd', p, v).reshape(bsz, seq, d)
    return o @ wo


def moe_ffn(x, w_router, router_bias, w_gate, w_up, w_down, sh_gate, sh_up, sh_down):
    bsz, seq, d = x.shape
    n_tok = bsz * seq
    xt = x.reshape(n_tok, d)
    scores = jax.nn.sigmoid((xt @ w_router).astype(jnp.float32))
    biased = scores + router_bias.astype(jnp.float32)
    grouped = biased.reshape(n_tok, N_EXPERT_GROUPS, N_EXPERTS // N_EXPERT_GROUPS)
    group_score = lax.top_k(grouped, 2)[0].sum(-1)
    top_groups = lax.top_k(group_score, TOPK_GROUPS)[1]
    group_mask = jax.nn.one_hot(top_groups, N_EXPERT_GROUPS, dtype=jnp.float32).sum(1) > 0
    masked = jnp.where(group_mask[:, :, None], grouped, -jnp.inf).reshape(n_tok, N_EXPERTS)
    top_idx = lax.top_k(masked, TOP_K)[1]
    gates = jnp.take_along_axis(scores, top_idx, axis=1)
    gates = ROUTED_SCALE * gates / jnp.sum(gates, -1, keepdims=True)
    n_assign = n_tok * TOP_K
    n_pad = -(-(n_assign + N_EXPERTS * MOE_BLOCK) // MOE_BLOCK) * MOE_BLOCK
    n_blocks = n_pad // MOE_BLOCK
    expert_flat = top_idx.reshape(-1)
    order = jnp.argsort(expert_flat)
    sorted_expert = expert_flat[order]
    sorted_tok = (order // TOP_K).astype(jnp.int32)
    sorted_gate = gates.reshape(-1)[order]
    counts = jnp.bincount(expert_flat, length=N_EXPERTS)
    starts = jnp.cumsum(counts) - counts
    padded = (counts + MOE_BLOCK - 1) // MOE_BLOCK * MOE_BLOCK
    pad_ends = jnp.cumsum(padded)
    pad_starts = pad_ends - padded
    dest = pad_starts[sorted_expert] + jnp.arange(n_assign) - starts[sorted_expert]
    pad_tok = jnp.full((n_pad,), n_tok, jnp.int32).at[dest].set(sorted_tok)
    pad_gate = jnp.zeros((n_pad,), x.dtype).at[dest].set(sorted_gate.astype(x.dtype))
    block_expert = jnp.minimum(
        jnp.searchsorted(pad_ends, jnp.arange(n_blocks) * MOE_BLOCK, side='right'), N_EXPERTS - 1)
    x_padded = jnp.concatenate([xt, jnp.zeros((1, d), x.dtype)], 0)

    def expert_block(args):
        e, tok = args
        xb = x_padded[tok]
        h = jax.nn.silu(xb @ w_gate[e]) * (xb @ w_up[e])
        return h @ w_down[e]

    y = lax.map(expert_block, (block_expert, pad_tok.reshape(n_blocks, MOE_BLOCK)))
    y = y.reshape(n_pad, d) * pad_gate[:, None]
    routed = jnp.zeros((n_tok + 1, d), x.dtype).at[pad_tok].add(y)[:n_tok]
    shared = (jax.nn.silu(xt @ sh_gate) * (xt @ sh_up)) @ sh_down
    return (routed + shared).reshape(bsz, seq, d)


def setup_inputs(seed: int = 0) -> dict:
    key = jax.random.key(seed)
    ks = iter(jax.random.split(key, 40))
    f32 = jnp.float32
    G, P, H = N_SSM_GROUPS, SSM_STATE, SSM_GROUP

    def nrm(shape, scale):
        return jax.random.normal(next(ks), shape, f32) * scale

    def gain(shape):
        return 1.0 + nrm(shape, 0.02)

    beta = DEEPNORM_BETA
    x = nrm((BATCH, SEQ, D_MODEL), 1.0)
    mem = nrm((BATCH, MEM_LEN, D_MODEL), 1.0)
    w_in = nrm((DEPTH, D_MODEL, 2 * D_CONV + D_SSM), D_MODEL ** -0.5)
    conv_w = nrm((DEPTH, CONV_WIDTH, D_CONV), CONV_WIDTH ** -0.5)
    conv_b = nrm((DEPTH, D_CONV), 0.02)
    conv_ln_g = gain((DEPTH, D_CONV))
    conv_ln_b = nrm((DEPTH, D_CONV), 0.02)
    ssm_log_dt = jax.random.uniform(next(ks), (DEPTH, G), f32, math.log(DT_MIN), math.log(DT_MAX))
    ssm_a_re = -0.5 + nrm((DEPTH, G, P), 0.01)
    ssm_a_im = jnp.pi * jnp.arange(P, dtype=f32) + nrm((DEPTH, G, P), 0.01)
    ssm_b_re = nrm((DEPTH, G, P, H), (2 * H) ** -0.5)
    ssm_b_im = nrm((DEPTH, G, P, H), (2 * H) ** -0.5)
    ssm_c_re = nrm((DEPTH, G, H, P), (2 * P) ** -0.5)
    ssm_c_im = nrm((DEPTH, G, H, P), (2 * P) ** -0.5)
    ssm_d = nrm((DEPTH, D_SSM), 1.0)
    w_glu = nrm((DEPTH, D_SSM, D_SSM), D_SSM ** -0.5)
    b_glu = nrm((DEPTH, D_SSM), 0.02)
    g_conv_out = gain((DEPTH, D_CONV))
    g_ssm_out = gain((DEPTH, D_SSM))
    w_out = nrm((DEPTH, D_MIX, D_MODEL), beta * D_MIX ** -0.5)
    ln1_g = gain((DEPTH, D_MODEL))
    ln1_b = nrm((DEPTH, D_MODEL), 0.02)
    wq = nrm((DEPTH, D_MODEL, D_MODEL), D_MODEL ** -0.5)
    wk = nrm((DEPTH, D_MODEL, D_MODEL), D_MODEL ** -0.5)
    wv = nrm((DEPTH, D_MODEL, D_MODEL), beta * D_MODEL ** -0.5)
    wo = nrm((DEPTH, D_MODEL, D_MODEL), beta * D_MODEL ** -0.5)
    ln2_g = gain((DEPTH, D_MODEL))
    ln2_b = nrm((DEPTH, D_MODEL), 0.02)
    w_router = nrm((DEPTH, D_MODEL, N_EXPERTS), D_MODEL ** -0.5)
    router_bias = nrm((DEPTH, N_EXPERTS), 0.01)
    w_gate = nrm((DEPTH, N_EXPERTS, D_MODEL, D_EXPERT), D_MODEL ** -0.5)
    w_up = nrm((DEPTH, N_EXPERTS, D_MODEL, D_EXPERT), D_MODEL ** -0.5)
    w_down = nrm((DEPTH, N_EXPERTS, D_EXPERT, D_MODEL), beta * D_EXPERT ** -0.5)
    sh_gate = nrm((DEPTH, D_MODEL, D_SHARED), D_MODEL ** -0.5)
    sh_up = nrm((DEPTH, D_MODEL, D_SHARED), D_MODEL ** -0.5)
    sh_down = nrm((DEPTH, D_SHARED, D_MODEL), beta * D_SHARED ** -0.5)
    ln3_g = gain((DEPTH, D_MODEL))
    ln3_b = nrm((DEPTH, D_MODEL), 0.02)
    return {'x': x, 'mem': mem, 'w_in': w_in, 'conv_w': conv_w, 'conv_b': conv_b,
            'conv_ln_g': conv_ln_g, 'conv_ln_b': conv_ln_b, 'ssm_log_dt': ssm_log_dt,
            'ssm_a_re': ssm_a_re, 'ssm_a_im': ssm_a_im, 'ssm_b_re': ssm_b_re, 'ssm_b_im': ssm_b_im,
            'ssm_c_re': ssm_c_re, 'ssm_c_im': ssm_c_im, 'ssm_d': ssm_d, 'w_glu': w_glu,
            'b_glu': b_glu, 'g_conv_out': g_conv_out, 'g_ssm_out': g_ssm_out, 'w_out': w_out,
            'ln1_g': ln1_g, 'ln1_b': ln1_b, 'wq': wq, 'wk': wk, 'wv': wv, 'wo': wo,
            'ln2_g': ln2_g, 'ln2_b': ln2_b, 'w_router': w_router, 'router_bias': router_bias,
            'w_gate': w_gate, 'w_up': w_up, 'w_down': w_down, 'sh_gate': sh_gate,
            'sh_up': sh_up, 'sh_down': sh_down, 'ln3_g': ln3_g, 'ln3_b': ln3_b}


def reference(x, mem, w_in, conv_w, conv_b, conv_ln_g, conv_ln_b, ssm_log_dt, ssm_a_re, ssm_a_im,
              ssm_b_re, ssm_b_im, ssm_c_re, ssm_c_im, ssm_d, w_glu, b_glu, g_conv_out, g_ssm_out,
              w_out, ln1_g, ln1_b, wq, wk, wv, wo, ln2_g, ln2_b, w_router, router_bias,
              w_gate, w_up, w_down, sh_gate, sh_up, sh_down, ln3_g, ln3_b):
    for l in range(DEPTH):
        mix = hybrid_mixer(x, w_in[l], conv_w[l], conv_b[l], conv_ln_g[l], conv_ln_b[l],
                           ssm_log_dt[l], ssm_a_re[l], ssm_a_im[l], ssm_b_re[l], ssm_b_im[l],
                           ssm_c_re[l], ssm_c_im[l], ssm_d[l], w_glu[l], b_glu[l],
                           g_conv_out[l], g_ssm_out[l], w_out[l])
        x = layer_norm(DEEPNORM_ALPHA * x + mix, ln1_g[l], ln1_b[l])
        xa = memory_cross_attention(x, mem, wq[l], wk[l], wv[l], wo[l])
        x = layer_norm(DEEPNORM_ALPHA * x + xa, ln2_g[l], ln2_b[l])
        ff = moe_ffn(x, w_router[l], router_bias[l], w_gate[l], w_up[l], w_down[l],
                     sh_gate[l], sh_up[l], sh_down[l])
        x = layer_norm(DEEPNORM_ALPHA * x + ff, ln3_g[l], ln3_b[l])
    return x
```

```python
import functools

import jax
import jax.numpy as jnp
from jax import lax
from jax.experimental import pallas as pl
from jax.experimental.pallas import tpu as pltpu

F32 = jnp.float32
BF16 = jnp.bfloat16

D_MODEL = 1024
D_CONV = 512
D_SSM = 512
CONV_WIDTH = 31
SSM_GROUP = 16
N_SSM_GROUPS = D_SSM // SSM_GROUP
SSM_STATE = 64
N_STATE = N_SSM_GROUPS * SSM_STATE
N_XHEADS = 4
XHEAD_DIM = D_MODEL // N_XHEADS
N_EXPERTS = 256
TOP_K = 8
N_EXPERT_GROUPS = 8
GROUP_SIZE = N_EXPERTS // N_EXPERT_GROUPS
TOPK_GROUPS = 4
D_EXPERT = 256
ROUTED_SCALE = 2.5
EPS = 1e-5

SUBLANES = 8
HALO_STEPS = 32
MIX_STEPS = 128
CONV_CHUNK = 32
SCAN_LANES = 512
ATTN_ROWS = 512
MOE_ROWS = 128
FFN_ROWS = 512
VMEM_LIMIT = 56 * 1024 * 1024


def _sigmoid(x):
    return 1.0 / (1.0 + jnp.exp(-x))


def _silu(x):
    return x * _sigmoid(x)


def _gelu_tanh(x):
    c = 0.7978845608028654
    return 0.5 * x * (1.0 + jnp.tanh(c * (x + 0.044715 * (x * x * x))))


def _layer_norm(x, g, b):
    mu = jnp.mean(x, -1, keepdims=True)
    xc = x - mu
    var = jnp.mean(xc * xc, -1, keepdims=True)
    return xc * lax.rsqrt(var + EPS) * g + b


def _rms_norm(x, g):
    return x * lax.rsqrt(jnp.mean(x * x, -1, keepdims=True) + EPS) * g


def _full_spec(shape):
    return pl.BlockSpec(shape, lambda *_: (0,) * len(shape))


def _mixer_kernel(x_ref, w_in_ref, cw_ref, cvec_ref, bband_ref, lam_ref, cband_ref, svec_ref,
                  wglu_ref, wout_ref, lnvec_ref, o_ref,
                  ha_ref, hb_ref, s_ref, carry_ref, mrg_ref, *, nb, alpha):
    rows = x_ref.shape[0]
    halo = HALO_STEPS * nb
    half = N_STATE // 2

    @pl.when(pl.program_id(0) == 0)
    def _():
        ha_ref[...] = jnp.zeros_like(ha_ref)
        hb_ref[...] = jnp.zeros_like(hb_ref)
        carry_ref[...] = jnp.zeros_like(carry_ref)

    x = x_ref[...]
    proj = jnp.dot(x.astype(BF16), w_in_ref[...], preferred_element_type=F32)

    h = proj[:, :D_CONV] * _sigmoid(proj[:, D_CONV:2 * D_CONV])
    ha_ref[halo:halo + rows, :] = h
    hb_ref[halo + nb:halo + nb + rows, :] = h

    def conv_chunk(c, carry):
        r0 = c * CONV_CHUNK
        acc = jnp.broadcast_to(cvec_ref[0:1, :], (CONV_CHUNK, D_CONV))
        for k in range(CONV_WIDTH):
            off = halo + (k - (CONV_WIDTH - 1)) * nb
            if off % SUBLANES == 0:
                slab = ha_ref[pl.ds(pl.multiple_of(r0 + off, SUBLANES), CONV_CHUNK), :]
            else:
                slab = hb_ref[pl.ds(pl.multiple_of(r0 + off + nb, SUBLANES), CONV_CHUNK), :]
            acc = acc + cw_ref[k:k + 1, :] * slab
        y = _silu(_layer_norm(acc, cvec_ref[1:2, :], cvec_ref[2:3, :]))
        y = _rms_norm(y, cvec_ref[3:4, :])
        mrg_ref[pl.ds(pl.multiple_of(r0, CONV_CHUNK), CONV_CHUNK), 0:D_CONV] = y.astype(BF16)
        return carry

    lax.fori_loop(0, rows // CONV_CHUNK, conv_chunk, 0)
    ha_ref[0:halo, :] = ha_ref[rows:rows + halo, :]
    hb_ref[0:halo + SUBLANES, :] = hb_ref[rows:rows + halo + SUBLANES, :]

    u = proj[:, 2 * D_CONV:]
    ub = u.astype(BF16)
    for j in range(2):
        s_ref[j] = jnp.dot(ub[:, j * (D_SSM // 2):(j + 1) * (D_SSM // 2)], bband_ref[j],
                           preferred_element_type=F32)

    first_step = lax.broadcasted_iota(jnp.int32, (SUBLANES, SCAN_LANES), 0) < nb
    for j in range(2):
        for c in range(half // SCAN_LANES):
            re_cols = slice(c * SCAN_LANES, (c + 1) * SCAN_LANES)
            im_cols = slice(half + c * SCAN_LANES, half + (c + 1) * SCAN_LANES)
            lr = jnp.broadcast_to(lam_ref[j:j + 1, re_cols], (SUBLANES, SCAN_LANES))
            li = jnp.broadcast_to(lam_ref[2 + j:3 + j, re_cols], (SUBLANES, SCAN_LANES))

            def scan_tile(i, prev, j=j, re_cols=re_cols, im_cols=im_cols, lr=lr, li=li):
                pre, pim = prev
                r0 = pl.multiple_of(i * SUBLANES, SUBLANES)
                bre = s_ref[j, pl.ds(r0, SUBLANES), re_cols]
                bim = s_ref[j, pl.ds(r0, SUBLANES), im_cols]
                are = lr * pre - li * pim + bre
                aim = lr * pim + li * pre + bim
                rre = pltpu.roll(are, nb, 0)
                rim = pltpu.roll(aim, nb, 0)
                sre = lr * rre - li * rim + bre
                sim = lr * rim + li * rre + bim
                s_ref[j, pl.ds(r0, SUBLANES), re_cols] = jnp.where(first_step, are, sre)
                s_ref[j, pl.ds(r0, SUBLANES), im_cols] = jnp.where(first_step, aim, sim)
                return pltpu.roll(sre, nb, 0), pltpu.roll(sim, nb, 0)

            prev0 = (carry_ref[j, :, re_cols], carry_ref[j, :, im_cols])
            pre, pim = lax.fori_loop(0, rows // SUBLANES, scan_tile, prev0)
            carry_ref[j, :, re_cols] = pre
            carry_ref[j, :, im_cols] = pim

    ys = [jnp.dot(s_ref[j].astype(BF16), cband_ref[j], preferred_element_type=F32) for j in range(2)]
    y = jnp.concatenate(ys, axis=-1) + svec_ref[0:1, :] * u
    z = _gelu_tanh(y)
    z = z * _sigmoid(jnp.dot(z.astype(BF16), wglu_ref[...], preferred_element_type=F32) + svec_ref[1:2, :])
    mrg_ref[:, D_CONV:] = _rms_norm(z, svec_ref[2:3, :]).astype(BF16)

    mix = jnp.dot(mrg_ref[...], wout_ref[...], preferred_element_type=F32)
    o_ref[...] = _layer_norm(alpha * x + mix, lnvec_ref[0:1, :], lnvec_ref[1:2, :])


def _ssm_bands(log_dt, a_re, a_im, b_re, b_im, c_re, c_im):
    g, p, hh = N_SSM_GROUPS, SSM_STATE, SSM_GROUP
    dt = jnp.exp(log_dt.astype(F32))[:, None]
    mag = jnp.exp(a_re * dt)
    ang = a_im * dt
    lb_re = mag * jnp.cos(ang)
    lb_im = mag * jnp.sin(ang)
    den = a_re * a_re + a_im * a_im
    coef_re = ((lb_re - 1.0) * a_re + lb_im * a_im) / den
    coef_im = (lb_im * a_re - (lb_re - 1.0) * a_im) / den
    bb_re = coef_re[..., None] * b_re - coef_im[..., None] * b_im
    bb_im = coef_re[..., None] * b_im + coef_im[..., None] * b_re
    eye = jnp.eye(g // 2, dtype=F32)

    def in_band(bb, j):
        blk = bb[j * (g // 2):(j + 1) * (g // 2)]
        return jnp.einsum('gph,gk->ghkp', blk, eye).reshape(g // 2 * hh, g // 2 * p)

    def out_band(cc, j):
        blk = cc[j * (g // 2):(j + 1) * (g // 2)]
        return jnp.einsum('ghp,gk->gpkh', blk, eye).reshape(g // 2 * p, g // 2 * hh)

    bband = jnp.stack([jnp.concatenate([in_band(bb_re, j), in_band(bb_im, j)], axis=1) for j in range(2)])
    cband = jnp.stack([jnp.concatenate([out_band(c_re, j), -out_band(c_im, j)], axis=0) for j in range(2)])
    lam = jnp.concatenate([lb_re.reshape(2, -1), lb_im.reshape(2, -1)], axis=0)
    return bband.astype(BF16), cband.astype(BF16), lam


def _mixer(x_rows, nb, alpha, w_in, conv_w, conv_b, conv_ln_g, conv_ln_b, log_dt, a_re, a_im, b_re, b_im,
           c_re, c_im, d_skip, w_glu, b_glu, g_conv_out, g_ssm_out, w_out, ln_g, ln_b):
    n_rows, d = x_rows.shape
    steps = n_rows // nb
    tile_steps = min(MIX_STEPS, steps)
    rows = tile_steps * nb
    assert nb * 2 == SUBLANES and steps % tile_steps == 0 and tile_steps >= HALO_STEPS
    assert rows % CONV_CHUNK == 0
    halo = HALO_STEPS * nb
    bband, cband, lam = _ssm_bands(log_dt, a_re, a_im, b_re, b_im, c_re, c_im)
    cvec = jnp.stack([conv_b, conv_ln_g, conv_ln_b, g_conv_out])
    svec = jnp.stack([d_skip, b_glu, g_ssm_out])
    lnvec = jnp.stack([ln_g, ln_b])
    operands = (x_rows, w_in.astype(BF16), conv_w, cvec, bband, lam, cband, svec,
                w_glu.astype(BF16), w_out.astype(BF16), lnvec)
    in_specs = [pl.BlockSpec((rows, d), lambda i: (i, 0))] + [_full_spec(a.shape) for a in operands[1:]]
    return pl.pallas_call(
        functools.partial(_mixer_kernel, nb=nb, alpha=alpha),
        out_shape=jax.ShapeDtypeStruct((n_rows, d), F32),
        grid=(n_rows // rows,),
        in_specs=in_specs,
        out_specs=pl.BlockSpec((rows, d), lambda i: (i, 0)),
        scratch_shapes=[
            pltpu.VMEM((halo + rows, D_CONV), F32),
            pltpu.VMEM((halo + SUBLANES + rows, D_CONV), F32),
            pltpu.VMEM((2, rows, N_STATE), F32),
            pltpu.VMEM((2, SUBLANES, N_STATE), F32),
            pltpu.VMEM((rows, D_CONV + D_SSM), BF16),
        ],
        compiler_params=pltpu.CompilerParams(dimension_semantics=("arbitrary",),
                                             vmem_limit_bytes=VMEM_LIMIT),
        name="mixer",
    )(*operands)


def _kv_kernel(mem_ref, wk_ref, wv_ref, k_ref, v_ref):
    m = mem_ref[0].astype(BF16)
    k_ref[0] = jnp.dot(m, wk_ref[...], preferred_element_type=F32).astype(BF16)
    v_ref[0] = jnp.dot(m, wv_ref[...], preferred_element_type=F32).astype(BF16)


def _kv_proj(mem, wk, wv):
    b, m, d = mem.shape
    blk = pl.BlockSpec((1, m, d), lambda i: (i, 0, 0))
    return pl.pallas_call(
        _kv_kernel,
        out_shape=(jax.ShapeDtypeStruct((b, m, d), BF16), jax.ShapeDtypeStruct((b, m, d), BF16)),
        grid=(b,),
        in_specs=[blk, _full_spec(wk.shape), _full_spec(wv.shape)],
        out_specs=(blk, blk),
        compiler_params=pltpu.CompilerParams(dimension_semantics=("arbitrary",),
                                             vmem_limit_bytes=VMEM_LIMIT),
        name="kv_proj",
    )(mem, wk.astype(BF16), wv.astype(BF16))


def _route(x2, wr_ref, rb_ref):
    t = x2.shape[0]
    logits = lax.dot_general(wr_ref[...], x2.astype(BF16), (((1,), (1,)), ((), ())),
                             preferred_element_type=F32)
    scores = _sigmoid(logits)
    biased = scores + rb_ref[...]
    neg = -jnp.inf

    in_group = lax.broadcasted_iota(jnp.int32, (GROUP_SIZE, t), 0).astype(F32)
    slabs, group_rows = [], []
    for g in range(N_EXPERT_GROUPS):
        slab = biased[g * GROUP_SIZE:(g + 1) * GROUP_SIZE, :]
        m1 = jnp.max(slab, axis=0, keepdims=True)
        i1 = jnp.min(jnp.where(slab == m1, in_group, float(GROUP_SIZE)), axis=0, keepdims=True)
        m2 = jnp.max(jnp.where(in_group == i1, neg, slab), axis=0, keepdims=True)
        slabs.append(slab)
        group_rows.append(m1 + m2)
    group_score = jnp.concatenate(group_rows, axis=0)

    gid = lax.broadcasted_iota(jnp.int32, (N_EXPERT_GROUPS, t), 0).astype(F32)
    chosen = jnp.zeros((N_EXPERT_GROUPS, t), F32)
    for _ in range(TOPK_GROUPS):
        m = jnp.max(group_score, axis=0, keepdims=True)
        gi = jnp.min(jnp.where(group_score == m, gid, float(N_EXPERT_GROUPS)), axis=0, keepdims=True)
        hit = gid == gi
        chosen = jnp.where(hit, 1.0, chosen)
        group_score = jnp.where(hit, neg, group_score)

    masked = jnp.concatenate(
        [jnp.where(chosen[g:g + 1, :] > 0.5, slabs[g], neg) for g in range(N_EXPERT_GROUPS)], axis=0)

    eid = lax.broadcasted_iota(jnp.int32, (N_EXPERTS, t), 0).astype(F32)
    ids, gates = [], []
    for _ in range(TOP_K):
        m = jnp.max(masked, axis=0, keepdims=True)
        ei = jnp.min(jnp.where(masked == m, eid, float(N_EXPERTS)), axis=0, keepdims=True)
        hit = eid == ei
        gates.append(jnp.sum(jnp.where(hit, scores, 0.0), axis=0, keepdims=True))
        ids.append(ei)
        masked = jnp.where(hit, neg, masked)
    ids = jnp.concatenate(ids, axis=0).astype(jnp.int32)
    gates = jnp.concatenate(gates, axis=0)
    gates = ROUTED_SCALE * gates / jnp.sum(gates, axis=0, keepdims=True)
    return ids, gates


def _attn_kernel(x_ref, k_ref, v_ref, wq_ref, wo_ref, lnvec_ref, wr_ref, rb_ref,
                 x2_ref, ids_ref, gates_ref, *, alpha):
    x = x_ref[0]
    q = jnp.dot(x.astype(BF16), wq_ref[...], preferred_element_type=F32)
    heads = []
    for h in range(N_XHEADS):
        cols = slice(h * XHEAD_DIM, (h + 1) * XHEAD_DIM)
        s = lax.dot_general(q[:, cols].astype(BF16), k_ref[0, :, cols], (((1,), (1,)), ((), ())),
                            preferred_element_type=F32) * (XHEAD_DIM ** -0.5)
        p = jnp.exp(s - jnp.max(s, axis=-1, keepdims=True))
        p = p / jnp.sum(p, axis=-1, keepdims=True)
        heads.append(jnp.dot(p.astype(BF16), v_ref[0, :, cols], preferred_element_type=F32))
    o = jnp.concatenate(heads, axis=-1)
    xa = jnp.dot(o.astype(BF16), wo_ref[...], preferred_element_type=F32)
    x2 = _layer_norm(alpha * x + xa, lnvec_ref[0:1, :], lnvec_ref[1:2, :])
    x2_ref[0] = x2
    ids, gates = _route(x2, wr_ref, rb_ref)
    ids_ref[0] = ids
    gates_ref[0] = gates


def _attention_router(x, k, v, wq, wo, ln_g, ln_b, w_router, router_bias, alpha):
    b, l, d = x.shape
    m = k.shape[1]
    tq = min(ATTN_ROWS, l)
    assert l % tq == 0
    lnvec = jnp.stack([ln_g, ln_b])
    wr_t = w_router.T.astype(BF16)
    rb = router_bias.astype(F32)[:, None]
    tile = pl.BlockSpec((1, tq, d), lambda i, j: (i, j, 0))
    mem_blk = pl.BlockSpec((1, m, d), lambda i, j: (i, 0, 0))
    topk_blk = pl.BlockSpec((1, TOP_K, tq), lambda i, j: (i, 0, j))
    return pl.pallas_call(
        functools.partial(_attn_kernel, alpha=alpha),
        out_shape=(jax.ShapeDtypeStruct((b, l, d), F32),
                   jax.ShapeDtypeStruct((b, TOP_K, l), jnp.int32),
                   jax.ShapeDtypeStruct((b, TOP_K, l), F32)),
        grid=(b, l // tq),
        in_specs=[tile, mem_blk, mem_blk, _full_spec(wq.shape), _full_spec(wo.shape),
                  _full_spec(lnvec.shape), _full_spec(wr_t.shape), _full_spec(rb.shape)],
        out_specs=(tile, topk_blk, topk_blk),
        compiler_params=pltpu.CompilerParams(dimension_semantics=("arbitrary", "arbitrary"),
                                             vmem_limit_bytes=VMEM_LIMIT),
        name="attention_router",
    )(x, k, v, wq.astype(BF16), wo.astype(BF16), lnvec, wr_t, rb)


def _moe_kernel(starts_ref, counts_ref, tok_ref, gate_ref, x_hbm, wg_ref, wu_ref, wd_ref, out_hbm,
                xs_ref, acc_ref, lhs_ref, y_ref, wgu_ref, wdn_ref, sem):
    part = pl.program_id(0)
    e = pl.program_id(1)
    n_e = pl.num_programs(1)
    n_tok = xs_ref.shape[0]

    @pl.when(e == 0)
    def _():
        load = pltpu.make_async_copy(x_hbm.at[part], xs_ref, sem.at[0])
        load.start()
        acc_ref[...] = jnp.zeros_like(acc_ref)
        lhs_ref[...] = jnp.zeros_like(lhs_ref)
        load.wait()

    start = starts_ref[part * n_e + e]
    count = counts_ref[part * n_e + e]
    end = start + count

    @pl.when(count > 0)
    def _():
        wgu_ref[:, :D_EXPERT] = wg_ref[0].astype(BF16)
        wgu_ref[:, D_EXPERT:] = wu_ref[0].astype(BF16)
        wdn_ref[...] = wd_ref[0].astype(BF16)

        def block(bi, carry):
            i0 = start + bi * MOE_ROWS
            n_groups = (jnp.minimum(MOE_ROWS, end - i0) + SUBLANES - 1) // SUBLANES

            def gather(g, c):
                base = pl.multiple_of(g * SUBLANES, SUBLANES)
                for r in range(SUBLANES):
                    t = tok_ref[0, 0, jnp.minimum(i0 + base + r, end - 1)]
                    lhs_ref[pl.ds(base + r, 1), :] = xs_ref[pl.ds(t, 1), :]
                return c

            lax.fori_loop(0, n_groups, gather, 0)
            gu = jnp.dot(lhs_ref[...].astype(BF16), wgu_ref[...], preferred_element_type=F32)
            hid = _silu(gu[:, :D_EXPERT]) * gu[:, D_EXPERT:]
            y_ref[...] = jnp.dot(hid.astype(BF16), wdn_ref[...], preferred_element_type=F32)

            def scatter(g, c):
                base = pl.multiple_of(g * SUBLANES, SUBLANES)
                toks, vals = [], []
                for r in range(SUBLANES):
                    idx = i0 + base + r
                    safe = jnp.minimum(idx, end - 1)
                    t = jnp.where(idx < end, tok_ref[0, 0, safe], n_tok + r)
                    toks.append(t)
                    vals.append(acc_ref[pl.ds(t, 1), :] + gate_ref[0, 0, safe] * y_ref[pl.ds(base + r, 1), :])
                for t, val in zip(toks, vals):
                    acc_ref[pl.ds(t, 1), :] = val
                return c

            lax.fori_loop(0, n_groups, scatter, 0)
            return carry

        lax.fori_loop(0, (count + MOE_ROWS - 1) // MOE_ROWS, block, 0)

    @pl.when(e == n_e - 1)
    def _():
        store = pltpu.make_async_copy(acc_ref.at[pl.ds(0, n_tok)], out_hbm.at[part], sem.at[1])
        store.start()
        store.wait()


def _routed_experts(x, tok_sorted, gate_sorted, starts, counts, w_gate, w_up, w_down):
    p, l, d = x.shape
    n_assign = tok_sorted.shape[-1]
    list_spec = pl.BlockSpec((1, 1, n_assign), lambda i, e, *_: (i, 0, 0), memory_space=pltpu.SMEM)
    grid_spec = pltpu.PrefetchScalarGridSpec(
        num_scalar_prefetch=2,
        grid=(p, N_EXPERTS),
        in_specs=[list_spec, list_spec,
                  pl.BlockSpec(memory_space=pl.ANY),
                  pl.BlockSpec((1, d, D_EXPERT), lambda i, e, *_: (e, 0, 0)),
                  pl.BlockSpec((1, d, D_EXPERT), lambda i, e, *_: (e, 0, 0)),
                  pl.BlockSpec((1, D_EXPERT, d), lambda i, e, *_: (e, 0, 0))],
        out_specs=pl.BlockSpec(memory_space=pl.ANY),
        scratch_shapes=[
            pltpu.VMEM((l, d), F32),
            pltpu.VMEM((l + SUBLANES, d), F32),
            pltpu.VMEM((MOE_ROWS, d), F32),
            pltpu.VMEM((MOE_ROWS, d), F32),
            pltpu.VMEM((d, 2 * D_EXPERT), BF16),
            pltpu.VMEM((D_EXPERT, d), BF16),
            pltpu.SemaphoreType.DMA((2,)),
        ])
    return pl.pallas_call(
        _moe_kernel,
        out_shape=jax.ShapeDtypeStruct((p, l, d), F32),
        grid_spec=grid_spec,
        compiler_params=pltpu.CompilerParams(dimension_semantics=("arbitrary", "arbitrary"),
                                             vmem_limit_bytes=VMEM_LIMIT),
        name="routed_experts",
    )(starts, counts, tok_sorted, gate_sorted, x, w_gate, w_up, w_down)


def _group_by_expert(ids, gates):
    p, k, l = ids.shape
    flat_e = jnp.swapaxes(ids, 1, 2).reshape(p, l * k)
    flat_g = jnp.swapaxes(gates, 1, 2).reshape(p, l * k)
    order = jnp.argsort(flat_e, axis=-1, stable=True)
    tok_sorted = (order // k).astype(jnp.int32)
    gate_sorted = jnp.take_along_axis(flat_g, order, axis=-1)
    counts = jnp.sum(flat_e[:, :, None] == jnp.arange(N_EXPERTS, dtype=jnp.int32), axis=1, dtype=jnp.int32)
    starts = jnp.cumsum(counts, axis=-1) - counts
    return (tok_sorted[:, None, :], gate_sorted[:, None, :],
            starts.reshape(-1).astype(jnp.int32), counts.reshape(-1))


def _ffn_out_kernel(x_ref, routed_ref, wgu_ref, wdn_ref, lnvec_ref, o_ref, *, alpha):
    x = x_ref[...]
    gu = jnp.dot(x.astype(BF16), wgu_ref[...], preferred_element_type=F32)
    d_sh = wdn_ref.shape[0]
    hid = _silu(gu[:, :d_sh]) * gu[:, d_sh:]
    shared = jnp.dot(hid.astype(BF16), wdn_ref[...], preferred_element_type=F32)
    o_ref[...] = _layer_norm(alpha * x + routed_ref[...] + shared, lnvec_ref[0:1, :], lnvec_ref[1:2, :])


def _ffn_out(x, routed, sh_gate, sh_up, sh_down, ln_g, ln_b, alpha):
    n, d = x.shape
    rows = min(FFN_ROWS, n)
    assert n % rows == 0
    wgu = jnp.concatenate([sh_gate, sh_up], axis=1).astype(BF16)
    lnvec = jnp.stack([ln_g, ln_b])
    tile = pl.BlockSpec((rows, d), lambda i: (i, 0))
    return pl.pallas_call(
        functools.partial(_ffn_out_kernel, alpha=alpha),
        out_shape=jax.ShapeDtypeStruct((n, d), F32),
        grid=(n // rows,),
        in_specs=[tile, tile, _full_spec(wgu.shape), _full_spec(sh_down.shape), _full_spec(lnvec.shape)],
        out_specs=tile,
        compiler_params=pltpu.CompilerParams(dimension_semantics=("arbitrary",),
                                             vmem_limit_bytes=VMEM_LIMIT),
        name="shared_ffn_norm",
    )(x, routed, wgu, sh_down.astype(BF16), lnvec)


def kernel(x, mem, w_in, conv_w, conv_b, conv_ln_g, conv_ln_b, ssm_log_dt, ssm_a_re, ssm_a_im, ssm_b_re, ssm_b_im, ssm_c_re, ssm_c_im, ssm_d, w_glu, b_glu, g_conv_out, g_ssm_out, w_out, ln1_g, ln1_b, wq, wk, wv, wo, ln2_g, ln2_b, w_router, router_bias, w_gate, w_up, w_down, sh_gate, sh_up, sh_down, ln3_g, ln3_b):
    depth = w_in.shape[0]
    alpha = (2 * depth) ** 0.25
    b, l, d = x.shape
    for i in range(depth):
        x_rows = jnp.swapaxes(x, 0, 1).reshape(l * b, d)
        x_rows = _mixer(x_rows, b, alpha, w_in[i], conv_w[i], conv_b[i], conv_ln_g[i], conv_ln_b[i],
                        ssm_log_dt[i], ssm_a_re[i], ssm_a_im[i], ssm_b_re[i], ssm_b_im[i],
                        ssm_c_re[i], ssm_c_im[i], ssm_d[i], w_glu[i], b_glu[i],
                        g_conv_out[i], g_ssm_out[i], w_out[i], ln1_g[i], ln1_b[i])
        x = jnp.swapaxes(x_rows.reshape(l, b, d), 0, 1)
        k, v = _kv_proj(mem, wk[i], wv[i])
        x, ids, gates = _attention_router(x, k, v, wq[i], wo[i], ln2_g[i], ln2_b[i],
                                          w_router[i], router_bias[i], alpha)
        tok_sorted, gate_sorted, starts, counts = _group_by_expert(ids, gates)
        routed = _routed_experts(x, tok_sorted, gate_sorted, starts, counts, w_gate[i], w_up[i], w_down[i])
        x = _ffn_out(x.reshape(b * l, d), routed.reshape(b * l, d), sh_gate[i], sh_up[i], sh_down[i],
                     ln3_g[i], ln3_b[i], alpha).reshape(b, l, d)
    return x
```

```python
import functools

import jax
import jax.numpy as jnp
from jax import lax
from jax.experimental import pallas as pl
from jax.experimental.pallas import tpu as pltpu

F32 = jnp.float32
BF16 = jnp.bfloat16

D_MODEL = 1024
D_CONV = 512
D_SSM = 512
CONV_WIDTH = 31
SSM_GROUP = 16
N_SSM_GROUPS = D_SSM // SSM_GROUP
SSM_STATE = 64
N_STATE = N_SSM_GROUPS * SSM_STATE
N_XHEADS = 4
XHEAD_DIM = D_MODEL // N_XHEADS
N_EXPERTS = 256
TOP_K = 8
N_EXPERT_GROUPS = 8
GROUP_SIZE = N_EXPERTS // N_EXPERT_GROUPS
TOPK_GROUPS = 4
D_EXPERT = 256
ROUTED_SCALE = 2.5
EPS = 1e-5

SUBLANES = 8
LANES = 128
TOKEN_TILE_ROWS = D_MODEL // LANES
HALO_STEPS = 32
MIX_STEPS = 128
CONV_CHUNK = 32
SCAN_LANES = 512
ATTN_ROWS = 512
MOE_ROWS = 256
FFN_ROWS = 512
VMEM_LIMIT = 56 * 1024 * 1024


def _sigmoid(x):
    return 1.0 / (1.0 + jnp.exp(-x))


def _silu(x):
    return x * _sigmoid(x)


def _gelu_tanh(x):
    c = 0.7978845608028654
    return 0.5 * x * (1.0 + jnp.tanh(c * (x + 0.044715 * (x * x * x))))


def _layer_norm(x, g, b):
    mu = jnp.mean(x, -1, keepdims=True)
    xc = x - mu
    var = jnp.mean(xc * xc, -1, keepdims=True)
    return xc * lax.rsqrt(var + EPS) * g + b


def _rms_norm(x, g):
    return x * lax.rsqrt(jnp.mean(x * x, -1, keepdims=True) + EPS) * g


def _full_spec(shape):
    return pl.BlockSpec(shape, lambda *_: (0,) * len(shape))


def _mixer_kernel(x_ref, w_in_ref, cw_ref, cvec_ref, bband_ref, lam_ref, cband_ref, svec_ref,
                  wglu_ref, wout_ref, lnvec_ref, o_ref,
                  ha_ref, hb_ref, s_ref, carry_ref, mrg_ref, *, nb, alpha):
    rows = x_ref.shape[0]
    halo = HALO_STEPS * nb
    half = N_STATE // 2

    @pl.when(pl.program_id(0) == 0)
    def _():
        ha_ref[...] = jnp.zeros_like(ha_ref)
        hb_ref[...] = jnp.zeros_like(hb_ref)
        carry_ref[...] = jnp.zeros_like(carry_ref)

    x = x_ref[...]
    proj = jnp.dot(x.astype(BF16), w_in_ref[...], preferred_element_type=F32)

    h = proj[:, :D_CONV] * _sigmoid(proj[:, D_CONV:2 * D_CONV])
    ha_ref[halo:halo + rows, :] = h
    hb_ref[halo + nb:halo + nb + rows, :] = h

    def conv_chunk(c, carry):
        r0 = c * CONV_CHUNK
        acc = jnp.broadcast_to(cvec_ref[0:1, :], (CONV_CHUNK, D_CONV))
        for k in range(CONV_WIDTH):
            off = halo + (k - (CONV_WIDTH - 1)) * nb
            if off % SUBLANES == 0:
                slab = ha_ref[pl.ds(pl.multiple_of(r0 + off, SUBLANES), CONV_CHUNK), :]
            else:
                slab = hb_ref[pl.ds(pl.multiple_of(r0 + off + nb, SUBLANES), CONV_CHUNK), :]
            acc = acc + cw_ref[k:k + 1, :] * slab
        y = _silu(_layer_norm(acc, cvec_ref[1:2, :], cvec_ref[2:3, :]))
        y = _rms_norm(y, cvec_ref[3:4, :])
        mrg_ref[pl.ds(pl.multiple_of(r0, CONV_CHUNK), CONV_CHUNK), 0:D_CONV] = y.astype(BF16)
        return carry

    lax.fori_loop(0, rows // CONV_CHUNK, conv_chunk, 0)
    ha_ref[0:halo, :] = ha_ref[rows:rows + halo, :]
    hb_ref[0:halo + SUBLANES, :] = hb_ref[rows:rows + halo + SUBLANES, :]

    u = proj[:, 2 * D_CONV:]
    ub = u.astype(BF16)
    for j in range(2):
        s_ref[j] = jnp.dot(ub[:, j * (D_SSM // 2):(j + 1) * (D_SSM // 2)], bband_ref[j],
                           preferred_element_type=F32)

    first_step = lax.broadcasted_iota(jnp.int32, (SUBLANES, SCAN_LANES), 0) < nb
    for j in range(2):
        for c in range(half // SCAN_LANES):
            re_cols = slice(c * SCAN_LANES, (c + 1) * SCAN_LANES)
            im_cols = slice(half + c * SCAN_LANES, half + (c + 1) * SCAN_LANES)
            lr = jnp.broadcast_to(lam_ref[j:j + 1, re_cols], (SUBLANES, SCAN_LANES))
            li = jnp.broadcast_to(lam_ref[2 + j:3 + j, re_cols], (SUBLANES, SCAN_LANES))

            def scan_tile(i, prev, j=j, re_cols=re_cols, im_cols=im_cols, lr=lr, li=li):
                pre, pim = prev
                r0 = pl.multiple_of(i * SUBLANES, SUBLANES)
                bre = s_ref[j, pl.ds(r0, SUBLANES), re_cols]
                bim = s_ref[j, pl.ds(r0, SUBLANES), im_cols]
                are = lr * pre - li * pim + bre
                aim = lr * pim + li * pre + bim
                rre = pltpu.roll(are, nb, 0)
                rim = pltpu.roll(aim, nb, 0)
                sre = lr * rre - li * rim + bre
                sim = lr * rim + li * rre + bim
                s_ref[j, pl.ds(r0, SUBLANES), re_cols] = jnp.where(first_step, are, sre)
                s_ref[j, pl.ds(r0, SUBLANES), im_cols] = jnp.where(first_step, aim, sim)
                return pltpu.roll(sre, nb, 0), pltpu.roll(sim, nb, 0)

            prev0 = (carry_ref[j, :, re_cols], carry_ref[j, :, im_cols])
            pre, pim = lax.fori_loop(0, rows // SUBLANES, scan_tile, prev0)
            carry_ref[j, :, re_cols] = pre
            carry_ref[j, :, im_cols] = pim

    ys = [jnp.dot(s_ref[j].astype(BF16), cband_ref[j], preferred_element_type=F32) for j in range(2)]
    y = jnp.concatenate(ys, axis=-1) + svec_ref[0:1, :] * u
    z = _gelu_tanh(y)
    z = z * _sigmoid(jnp.dot(z.astype(BF16), wglu_ref[...], preferred_element_type=F32) + svec_ref[1:2, :])
    mrg_ref[:, D_CONV:] = _rms_norm(z, svec_ref[2:3, :]).astype(BF16)

    mix = jnp.dot(mrg_ref[...], wout_ref[...], preferred_element_type=F32)
    o_ref[...] = _layer_norm(alpha * x + mix, lnvec_ref[0:1, :], lnvec_ref[1:2, :])


def _ssm_bands(log_dt, a_re, a_im, b_re, b_im, c_re, c_im):
    g, p, hh = N_SSM_GROUPS, SSM_STATE, SSM_GROUP
    dt = jnp.exp(log_dt.astype(F32))[:, None]
    mag = jnp.exp(a_re * dt)
    ang = a_im * dt
    lb_re = mag * jnp.cos(ang)
    lb_im = mag * jnp.sin(ang)
    den = a_re * a_re + a_im * a_im
    coef_re = ((lb_re - 1.0) * a_re + lb_im * a_im) / den
    coef_im = (lb_im * a_re - (lb_re - 1.0) * a_im) / den
    bb_re = coef_re[..., None] * b_re - coef_im[..., None] * b_im
    bb_im = coef_re[..., None] * b_im + coef_im[..., None] * b_re
    eye = jnp.eye(g // 2, dtype=F32)

    def in_band(bb, j):
        blk = bb[j * (g // 2):(j + 1) * (g // 2)]
        return jnp.einsum('gph,gk->ghkp', blk, eye).reshape(g // 2 * hh, g // 2 * p)

    def out_band(cc, j):
        blk = cc[j * (g // 2):(j + 1) * (g // 2)]
        return jnp.einsum('ghp,gk->gpkh', blk, eye).reshape(g // 2 * p, g // 2 * hh)

    bband = jnp.stack([jnp.concatenate([in_band(bb_re, j), in_band(bb_im, j)], axis=1) for j in range(2)])
    cband = jnp.stack([jnp.concatenate([out_band(c_re, j), -out_band(c_im, j)], axis=0) for j in range(2)])
    lam = jnp.concatenate([lb_re.reshape(2, -1), lb_im.reshape(2, -1)], axis=0)
    return bband.astype(BF16), cband.astype(BF16), lam


def _mixer(x_rows, nb, alpha, w_in, conv_w, conv_b, conv_ln_g, conv_ln_b, log_dt, a_re, a_im, b_re, b_im,
           c_re, c_im, d_skip, w_glu, b_glu, g_conv_out, g_ssm_out, w_out, ln_g, ln_b):
    n_rows, d = x_rows.shape
    steps = n_rows // nb
    tile_steps = min(MIX_STEPS, steps)
    rows = tile_steps * nb
    assert nb * 2 == SUBLANES and steps % tile_steps == 0 and tile_steps >= HALO_STEPS
    assert rows % CONV_CHUNK == 0
    halo = HALO_STEPS * nb
    bband, cband, lam = _ssm_bands(log_dt, a_re, a_im, b_re, b_im, c_re, c_im)
    cvec = jnp.stack([conv_b, conv_ln_g, conv_ln_b, g_conv_out])
    svec = jnp.stack([d_skip, b_glu, g_ssm_out])
    lnvec = jnp.stack([ln_g, ln_b])
    operands = (x_rows, w_in.astype(BF16), conv_w, cvec, bband, lam, cband, svec,
                w_glu.astype(BF16), w_out.astype(BF16), lnvec)
    in_specs = [pl.BlockSpec((rows, d), lambda i: (i, 0))] + [_full_spec(a.shape) for a in operands[1:]]
    return pl.pallas_call(
        functools.partial(_mixer_kernel, nb=nb, alpha=alpha),
        out_shape=jax.ShapeDtypeStruct((n_rows, d), F32),
        grid=(n_rows // rows,),
        in_specs=in_specs,
        out_specs=pl.BlockSpec((rows, d), lambda i: (i, 0)),
        scratch_shapes=[
            pltpu.VMEM((halo + rows, D_CONV), F32),
            pltpu.VMEM((halo + SUBLANES + rows, D_CONV), F32),
            pltpu.VMEM((2, rows, N_STATE), F32),
            pltpu.VMEM((2, SUBLANES, N_STATE), F32),
            pltpu.VMEM((rows, D_CONV + D_SSM), BF16),
        ],
        compiler_params=pltpu.CompilerParams(dimension_semantics=("arbitrary",),
                                             vmem_limit_bytes=VMEM_LIMIT),
        name="mixer",
    )(*operands)


def _kv_kernel(mem_ref, wk_ref, wv_ref, k_ref, v_ref):
    m = mem_ref[0].astype(BF16)
    k_ref[0] = jnp.dot(m, wk_ref[...], preferred_element_type=F32).astype(BF16)
    v_ref[0] = jnp.dot(m, wv_ref[...], preferred_element_type=F32).astype(BF16)


def _kv_proj(mem, wk, wv):
    b, m, d = mem.shape
    blk = pl.BlockSpec((1, m, d), lambda i: (i, 0, 0))
    return pl.pallas_call(
        _kv_kernel,
        out_shape=(jax.ShapeDtypeStruct((b, m, d), BF16), jax.ShapeDtypeStruct((b, m, d), BF16)),
        grid=(b,),
        in_specs=[blk, _full_spec(wk.shape), _full_spec(wv.shape)],
        out_specs=(blk, blk),
        compiler_params=pltpu.CompilerParams(dimension_semantics=("arbitrary",),
                                             vmem_limit_bytes=VMEM_LIMIT),
        name="kv_proj",
    )(mem, wk.astype(BF16), wv.astype(BF16))


def _route(x2, wr_ref, rb_ref):
    t = x2.shape[0]
    logits = lax.dot_general(wr_ref[...], x2.astype(BF16), (((1,), (1,)), ((), ())),
                             preferred_element_type=F32)
    scores = _sigmoid(logits)
    biased = scores + rb_ref[...]
    neg = -jnp.inf

    in_group = lax.broadcasted_iota(jnp.int32, (GROUP_SIZE, t), 0).astype(F32)
    slabs, group_rows = [], []
    for g in range(N_EXPERT_GROUPS):
        slab = biased[g * GROUP_SIZE:(g + 1) * GROUP_SIZE, :]
        m1 = jnp.max(slab, axis=0, keepdims=True)
        i1 = jnp.min(jnp.where(slab == m1, in_group, float(GROUP_SIZE)), axis=0, keepdims=True)
        m2 = jnp.max(jnp.where(in_group == i1, neg, slab), axis=0, keepdims=True)
        slabs.append(slab)
        group_rows.append(m1 + m2)
    group_score = jnp.concatenate(group_rows, axis=0)

    gid = lax.broadcasted_iota(jnp.int32, (N_EXPERT_GROUPS, t), 0).astype(F32)
    chosen = jnp.zeros((N_EXPERT_GROUPS, t), F32)
    for _ in range(TOPK_GROUPS):
        m = jnp.max(group_score, axis=0, keepdims=True)
        gi = jnp.min(jnp.where(group_score == m, gid, float(N_EXPERT_GROUPS)), axis=0, keepdims=True)
        hit = gid == gi
        chosen = jnp.where(hit, 1.0, chosen)
        group_score = jnp.where(hit, neg, group_score)

    masked = jnp.concatenate(
        [jnp.where(chosen[g:g + 1, :] > 0.5, slabs[g], neg) for g in range(N_EXPERT_GROUPS)], axis=0)

    eid = lax.broadcasted_iota(jnp.int32, (N_EXPERTS, t), 0).astype(F32)
    ids, gates = [], []
    for _ in range(TOP_K):
        m = jnp.max(masked, axis=0, keepdims=True)
        ei = jnp.min(jnp.where(masked == m, eid, float(N_EXPERTS)), axis=0, keepdims=True)
        hit = eid == ei
        gates.append(jnp.sum(jnp.where(hit, scores, 0.0), axis=0, keepdims=True))
        ids.append(ei)
        masked = jnp.where(hit, neg, masked)
    ids = jnp.concatenate(ids, axis=0).astype(jnp.int32)
    gates = jnp.concatenate(gates, axis=0)
    gates = ROUTED_SCALE * gates / jnp.sum(gates, axis=0, keepdims=True)
    return ids, gates


def _attn_kernel(x_ref, k_ref, v_ref, wq_ref, wo_ref, lnvec_ref, wr_ref, rb_ref,
                 x2_ref, xt_ref, ids_ref, gates_ref, *, alpha):
    x = x_ref[0]
    q = jnp.dot(x.astype(BF16), wq_ref[...], preferred_element_type=F32)
    heads = []
    for h in range(N_XHEADS):
        cols = slice(h * XHEAD_DIM, (h + 1) * XHEAD_DIM)
        s = lax.dot_general(q[:, cols].astype(BF16), k_ref[0, :, cols], (((1,), (1,)), ((), ())),
                            preferred_element_type=F32) * (XHEAD_DIM ** -0.5)
        p = jnp.exp(s - jnp.max(s, axis=-1, keepdims=True))
        p = p / jnp.sum(p, axis=-1, keepdims=True)
        heads.append(jnp.dot(p.astype(BF16), v_ref[0, :, cols], preferred_element_type=F32))
    o = jnp.concatenate(heads, axis=-1)
    xa = jnp.dot(o.astype(BF16), wo_ref[...], preferred_element_type=F32)
    x2 = _layer_norm(alpha * x + xa, lnvec_ref[0:1, :], lnvec_ref[1:2, :])
    x2_ref[0] = x2
    for s in range(TOKEN_TILE_ROWS):
        xt_ref[0, pl.ds(s, x2.shape[0], stride=TOKEN_TILE_ROWS), :] = x2[:, s * LANES:(s + 1) * LANES]
    ids, gates = _route(x2, wr_ref, rb_ref)
    ids_ref[0] = ids
    gates_ref[0] = gates


def _attention_router(x, k, v, wq, wo, ln_g, ln_b, w_router, router_bias, alpha):
    b, l, d = x.shape
    m = k.shape[1]
    tq = min(ATTN_ROWS, l)
    assert l % tq == 0
    lnvec = jnp.stack([ln_g, ln_b])
    wr_t = w_router.T.astype(BF16)
    rb = router_bias.astype(F32)[:, None]
    assert d == TOKEN_TILE_ROWS * LANES
    tile = pl.BlockSpec((1, tq, d), lambda i, j: (i, j, 0))
    tok_tile = pl.BlockSpec((1, tq * TOKEN_TILE_ROWS, LANES), lambda i, j: (i, j, 0))
    mem_blk = pl.BlockSpec((1, m, d), lambda i, j: (i, 0, 0))
    topk_blk = pl.BlockSpec((1, TOP_K, tq), lambda i, j: (i, 0, j))
    return pl.pallas_call(
        functools.partial(_attn_kernel, alpha=alpha),
        out_shape=(jax.ShapeDtypeStruct((b, l, d), F32),
                   jax.ShapeDtypeStruct((b, l * TOKEN_TILE_ROWS, LANES), F32),
                   jax.ShapeDtypeStruct((b, TOP_K, l), jnp.int32),
                   jax.ShapeDtypeStruct((b, TOP_K, l), F32)),
        grid=(b, l // tq),
        in_specs=[tile, mem_blk, mem_blk, _full_spec(wq.shape), _full_spec(wo.shape),
                  _full_spec(lnvec.shape), _full_spec(wr_t.shape), _full_spec(rb.shape)],
        out_specs=(tile, tok_tile, topk_blk, topk_blk),
        compiler_params=pltpu.CompilerParams(dimension_semantics=("arbitrary", "arbitrary"),
                                             vmem_limit_bytes=VMEM_LIMIT),
        name="attention_router",
    )(x, k, v, wq.astype(BF16), wo.astype(BF16), lnvec, wr_t, rb)


def _moe_kernel(starts_ref, counts_ref, tok_ref, gate_ref, x_hbm, wg_ref, wu_ref, wd_ref, out_hbm,
                xs_ref, acc_ref, lhs_ref, y_ref, wgu_ref, wdn_ref, sem):
    part = pl.program_id(0)
    e = pl.program_id(1)
    n_e = pl.num_programs(1)
    tr = TOKEN_TILE_ROWS
    n_rows = out_hbm.shape[1]

    def token_rows(t):
        return pl.ds(pl.multiple_of(t * tr, tr), tr)

    @pl.when(e == 0)
    def _():
        load = pltpu.make_async_copy(x_hbm.at[part], xs_ref.at[pl.ds(0, n_rows)], sem.at[0])
        load.start()
        acc_ref[...] = jnp.zeros_like(acc_ref)
        lhs_ref[...] = jnp.zeros_like(lhs_ref)
        xs_ref[pl.ds(n_rows, tr), :] = jnp.zeros((tr, LANES), F32)
        load.wait()

    start = starts_ref[part * n_e + e]
    count = counts_ref[part * n_e + e]

    def run_block(i0, n_groups, size):
        def gather(g, c):
            base = pl.multiple_of(g * SUBLANES, SUBLANES)
            for r in range(SUBLANES):
                t = tok_ref[0, 0, i0 + base + r]
                lhs_ref[token_rows(base + r), :] = xs_ref[token_rows(t), :]
            return c

        lax.fori_loop(0, n_groups, gather, 0)
        xb = jnp.concatenate([lhs_ref[pl.ds(s, size, stride=tr), :] for s in range(tr)], axis=-1)
        gu = jnp.dot(xb.astype(BF16), wgu_ref[...], preferred_element_type=F32)
        hid = _silu(gu[:, :D_EXPERT]) * gu[:, D_EXPERT:]
        y = jnp.dot(hid.astype(BF16), wdn_ref[...], preferred_element_type=F32)
        for s in range(tr):
            y_ref[pl.ds(s, size, stride=tr), :] = y[:, s * LANES:(s + 1) * LANES]

        def scatter(g, c):
            base = pl.multiple_of(g * SUBLANES, SUBLANES)
            rows, vals = [], []
            for r in range(SUBLANES):
                t = tok_ref[0, 0, i0 + base + r]
                rows.append(token_rows(t))
                vals.append(acc_ref[rows[-1], :] + gate_ref[0, 0, i0 + base + r] * y_ref[token_rows(base + r), :])
            for row, val in zip(rows, vals):
                acc_ref[row, :] = val
            return c

        lax.fori_loop(0, n_groups, scatter, 0)

    @pl.when(count > 0)
    def _():
        wgu_ref[:, :D_EXPERT] = wg_ref[0].astype(BF16)
        wgu_ref[:, D_EXPERT:] = wu_ref[0].astype(BF16)
        wdn_ref[...] = wd_ref[0].astype(BF16)
        n_full = count // MOE_ROWS

        def full_block(bi, c):
            run_block(start + bi * MOE_ROWS, MOE_ROWS // SUBLANES, MOE_ROWS)
            return c

        lax.fori_loop(0, n_full, full_block, 0)
        rest = count - n_full * MOE_ROWS
        rest_groups = (rest + SUBLANES - 1) // SUBLANES

        @pl.when(rest > MOE_ROWS // 2)
        def _():
            run_block(start + n_full * MOE_ROWS, rest_groups, MOE_ROWS)

        @pl.when((rest > 0) & (rest <= MOE_ROWS // 2))
        def _():
            run_block(start + n_full * MOE_ROWS, rest_groups, MOE_ROWS // 2)

    @pl.when(e == n_e - 1)
    def _():
        store = pltpu.make_async_copy(acc_ref.at[pl.ds(0, n_rows)], out_hbm.at[part], sem.at[1])
        store.start()
        store.wait()


def _routed_experts(xt, tok_list, gate_list, starts, counts, w_gate, w_up, w_down):
    p, n_rows, lanes = xt.shape
    d = w_gate.shape[1]
    n_list = tok_list.shape[-1]
    list_spec = pl.BlockSpec((1, 1, n_list), lambda i, e, *_: (i, 0, 0), memory_space=pltpu.SMEM)
    grid_spec = pltpu.PrefetchScalarGridSpec(
        num_scalar_prefetch=2,
        grid=(p, N_EXPERTS),
        in_specs=[list_spec, list_spec,
                  pl.BlockSpec(memory_space=pl.ANY),
                  pl.BlockSpec((1, d, D_EXPERT), lambda i, e, *_: (e, 0, 0)),
                  pl.BlockSpec((1, d, D_EXPERT), lambda i, e, *_: (e, 0, 0)),
                  pl.BlockSpec((1, D_EXPERT, d), lambda i, e, *_: (e, 0, 0))],
        out_specs=pl.BlockSpec(memory_space=pl.ANY),
        scratch_shapes=[
            pltpu.VMEM((n_rows + TOKEN_TILE_ROWS, lanes), F32),
            pltpu.VMEM((n_rows + TOKEN_TILE_ROWS, lanes), F32),
            pltpu.VMEM((MOE_ROWS * TOKEN_TILE_ROWS, lanes), F32),
            pltpu.VMEM((MOE_ROWS * TOKEN_TILE_ROWS, lanes), F32),
            pltpu.VMEM((d, 2 * D_EXPERT), BF16),
            pltpu.VMEM((D_EXPERT, d), BF16),
            pltpu.SemaphoreType.DMA((2,)),
        ])
    return pl.pallas_call(
        _moe_kernel,
        out_shape=jax.ShapeDtypeStruct((p, n_rows, lanes), F32),
        grid_spec=grid_spec,
        compiler_params=pltpu.CompilerParams(dimension_semantics=("arbitrary", "arbitrary"),
                                             vmem_limit_bytes=VMEM_LIMIT),
        name="routed_experts",
    )(starts, counts, tok_list, gate_list, xt, w_gate, w_up, w_down)


def _group_by_expert(ids, gates):
    p, k, l = ids.shape
    n_list = l * k + N_EXPERTS * SUBLANES

    def one_part(ids_p, gates_p):
        flat_e = ids_p.T.reshape(l * k)
        flat_g = gates_p.T.reshape(l * k)
        order = jnp.argsort(flat_e, stable=True)
        sorted_e = flat_e[order]
        counts = jnp.sum(flat_e[:, None] == jnp.arange(N_EXPERTS, dtype=jnp.int32), axis=0, dtype=jnp.int32)
        starts = jnp.cumsum(counts) - counts
        padded = (counts + SUBLANES - 1) // SUBLANES * SUBLANES
        pad_starts = jnp.cumsum(padded) - padded
        dest = pad_starts[sorted_e] + jnp.arange(l * k, dtype=jnp.int32) - starts[sorted_e]
        tok_list = jnp.full((n_list,), l, jnp.int32).at[dest].set((order // k).astype(jnp.int32))
        gate_list = jnp.zeros((n_list,), F32).at[dest].set(flat_g[order])
        return tok_list, gate_list, pad_starts.astype(jnp.int32), counts

    tok_list, gate_list, starts, counts = jax.vmap(one_part)(ids, gates)
    return tok_list[:, None, :], gate_list[:, None, :], starts.reshape(-1), counts.reshape(-1)


def _ffn_out_kernel(x_ref, routed_ref, wgu_ref, wdn_ref, lnvec_ref, o_ref, *, alpha):
    x = x_ref[...]
    rows = x.shape[0]
    routed = jnp.concatenate([routed_ref[pl.ds(s, rows, stride=TOKEN_TILE_ROWS), :]
                              for s in range(TOKEN_TILE_ROWS)], axis=-1)
    gu = jnp.dot(x.astype(BF16), wgu_ref[...], preferred_element_type=F32)
    d_sh = wdn_ref.shape[0]
    hid = _silu(gu[:, :d_sh]) * gu[:, d_sh:]
    shared = jnp.dot(hid.astype(BF16), wdn_ref[...], preferred_element_type=F32)
    o_ref[...] = _layer_norm(alpha * x + routed + shared, lnvec_ref[0:1, :], lnvec_ref[1:2, :])


def _ffn_out(x, routed_tiles, sh_gate, sh_up, sh_down, ln_g, ln_b, alpha):
    n, d = x.shape
    rows = min(FFN_ROWS, n)
    assert n % rows == 0
    wgu = jnp.concatenate([sh_gate, sh_up], axis=1).astype(BF16)
    lnvec = jnp.stack([ln_g, ln_b])
    tile = pl.BlockSpec((rows, d), lambda i: (i, 0))
    tok_tile = pl.BlockSpec((rows * TOKEN_TILE_ROWS, LANES), lambda i: (i, 0))
    return pl.pallas_call(
        functools.partial(_ffn_out_kernel, alpha=alpha),
        out_shape=jax.ShapeDtypeStruct((n, d), F32),
        grid=(n // rows,),
        in_specs=[tile, tok_tile, _full_spec(wgu.shape), _full_spec(sh_down.shape), _full_spec(lnvec.shape)],
        out_specs=tile,
        compiler_params=pltpu.CompilerParams(dimension_semantics=("arbitrary",),
                                             vmem_limit_bytes=VMEM_LIMIT),
        name="shared_ffn_norm",
    )(x, routed_tiles, wgu, sh_down.astype(BF16), lnvec)


def kernel(x, mem, w_in, conv_w, conv_b, conv_ln_g, conv_ln_b, ssm_log_dt, ssm_a_re, ssm_a_im, ssm_b_re, ssm_b_im, ssm_c_re, ssm_c_im, ssm_d, w_glu, b_glu, g_conv_out, g_ssm_out, w_out, ln1_g, ln1_b, wq, wk, wv, wo, ln2_g, ln2_b, w_router, router_bias, w_gate, w_up, w_down, sh_gate, sh_up, sh_down, ln3_g, ln3_b):
    depth = w_in.shape[0]
    alpha = (2 * depth) ** 0.25
    b, l, d = x.shape
    for i in range(depth):
        x_rows = jnp.swapaxes(x, 0, 1).reshape(l * b, d)
        x_rows = _mixer(x_rows, b, alpha, w_in[i], conv_w[i], conv_b[i], conv_ln_g[i], conv_ln_b[i],
                        ssm_log_dt[i], ssm_a_re[i], ssm_a_im[i], ssm_b_re[i], ssm_b_im[i],
                        ssm_c_re[i], ssm_c_im[i], ssm_d[i], w_glu[i], b_glu[i],
                        g_conv_out[i], g_ssm_out[i], w_out[i], ln1_g[i], ln1_b[i])
        x = jnp.swapaxes(x_rows.reshape(l, b, d), 0, 1)
        k, v = _kv_proj(mem, wk[i], wv[i])
        x, xt, ids, gates = _attention_router(x, k, v, wq[i], wo[i], ln2_g[i], ln2_b[i],
                                              w_router[i], router_bias[i], alpha)
        tok_list, gate_list, starts, counts = _group_by_expert(ids, gates)
        routed = _routed_experts(xt, tok_list, gate_list, starts, counts, w_gate[i], w_up[i], w_down[i])
        x = _ffn_out(x.reshape(b * l, d), routed.reshape(b * l * TOKEN_TILE_ROWS, LANES),
                     sh_gate[i], sh_up[i], sh_down[i], ln3_g[i], ln3_b[i], alpha).reshape(b, l, d)
    return x
```

```python
import functools

import jax
import jax.numpy as jnp
from jax import lax
from jax.experimental import pallas as pl
from jax.experimental.pallas import tpu as pltpu

F32 = jnp.float32
BF16 = jnp.bfloat16

D_MODEL = 1024
D_CONV = 512
D_SSM = 512
CONV_WIDTH = 31
SSM_GROUP = 16
N_SSM_GROUPS = D_SSM // SSM_GROUP
SSM_STATE = 64
N_STATE = N_SSM_GROUPS * SSM_STATE
N_XHEADS = 4
XHEAD_DIM = D_MODEL // N_XHEADS
N_EXPERTS = 256
TOP_K = 8
N_EXPERT_GROUPS = 8
GROUP_SIZE = N_EXPERTS // N_EXPERT_GROUPS
TOPK_GROUPS = 4
D_EXPERT = 256
ROUTED_SCALE = 2.5
EPS = 1e-5

SUBLANES = 8
LANES = 128
TOKEN_TILE_ROWS = D_MODEL // LANES
HALO_STEPS = 32
MIX_STEPS = 128
CONV_CHUNK = 32
SCAN_LANES = 512
ATTN_ROWS = 512
MOE_ROWS = 256
FFN_ROWS = 512
VMEM_LIMIT = 56 * 1024 * 1024


def _sigmoid(x):
    return 1.0 / (1.0 + jnp.exp(-x))


def _silu(x):
    return x * _sigmoid(x)


def _gelu_tanh(x):
    c = 0.7978845608028654
    return 0.5 * x * (1.0 + jnp.tanh(c * (x + 0.044715 * (x * x * x))))


def _layer_norm(x, g, b):
    mu = jnp.mean(x, -1, keepdims=True)
    xc = x - mu
    var = jnp.mean(xc * xc, -1, keepdims=True)
    return xc * lax.rsqrt(var + EPS) * g + b


def _rms_norm(x, g):
    return x * lax.rsqrt(jnp.mean(x * x, -1, keepdims=True) + EPS) * g


def _full_spec(shape):
    return pl.BlockSpec(shape, lambda *_: (0,) * len(shape))


def _mixer_kernel(x_ref, w_in_ref, cw_ref, cvec_ref, bband_ref, lam_ref, cband_ref, svec_ref,
                  wglu_ref, wout_ref, lnvec_ref, o_ref,
                  ha_ref, hb_ref, s_ref, carry_ref, mrg_ref, *, nb, alpha):
    rows = x_ref.shape[0]
    halo = HALO_STEPS * nb
    half = N_STATE // 2

    @pl.when(pl.program_id(0) == 0)
    def _():
        ha_ref[...] = jnp.zeros_like(ha_ref)
        hb_ref[...] = jnp.zeros_like(hb_ref)
        carry_ref[...] = jnp.zeros_like(carry_ref)

    x = x_ref[...]
    proj = jnp.dot(x.astype(BF16), w_in_ref[...], preferred_element_type=F32)

    h = proj[:, :D_CONV] * _sigmoid(proj[:, D_CONV:2 * D_CONV])
    ha_ref[halo:halo + rows, :] = h
    hb_ref[halo + nb:halo + nb + rows, :] = h

    def conv_chunk(c, carry):
        r0 = c * CONV_CHUNK
        acc = jnp.broadcast_to(cvec_ref[0:1, :], (CONV_CHUNK, D_CONV))
        for k in range(CONV_WIDTH):
            off = halo + (k - (CONV_WIDTH - 1)) * nb
            if off % SUBLANES == 0:
                slab = ha_ref[pl.ds(pl.multiple_of(r0 + off, SUBLANES), CONV_CHUNK), :]
            else:
                slab = hb_ref[pl.ds(pl.multiple_of(r0 + off + nb, SUBLANES), CONV_CHUNK), :]
            acc = acc + cw_ref[k:k + 1, :] * slab
        y = _silu(_layer_norm(acc, cvec_ref[1:2, :], cvec_ref[2:3, :]))
        y = _rms_norm(y, cvec_ref[3:4, :])
        mrg_ref[pl.ds(pl.multiple_of(r0, CONV_CHUNK), CONV_CHUNK), 0:D_CONV] = y.astype(BF16)
        return carry

    lax.fori_loop(0, rows // CONV_CHUNK, conv_chunk, 0)
    ha_ref[0:halo, :] = ha_ref[rows:rows + halo, :]
    hb_ref[0:halo + SUBLANES, :] = hb_ref[rows:rows + halo + SUBLANES, :]

    u = proj[:, 2 * D_CONV:]
    ub = u.astype(BF16)
    for j in range(2):
        s_ref[j] = jnp.dot(ub[:, j * (D_SSM // 2):(j + 1) * (D_SSM // 2)], bband_ref[j],
                           preferred_element_type=F32)

    first_step = lax.broadcasted_iota(jnp.int32, (SUBLANES, SCAN_LANES), 0) < nb
    for j in range(2):
        for c in range(half // SCAN_LANES):
            re_cols = slice(c * SCAN_LANES, (c + 1) * SCAN_LANES)
            im_cols = slice(half + c * SCAN_LANES, half + (c + 1) * SCAN_LANES)
            lr = jnp.broadcast_to(lam_ref[j:j + 1, re_cols], (SUBLANES, SCAN_LANES))
            li = jnp.broadcast_to(lam_ref[2 + j:3 + j, re_cols], (SUBLANES, SCAN_LANES))

            def scan_tile(i, prev, j=j, re_cols=re_cols, im_cols=im_cols, lr=lr, li=li):
                pre, pim = prev
                r0 = pl.multiple_of(i * SUBLANES, SUBLANES)
                bre = s_ref[j, pl.ds(r0, SUBLANES), re_cols]
                bim = s_ref[j, pl.ds(r0, SUBLANES), im_cols]
                are = lr * pre - li * pim + bre
                aim = lr * pim + li * pre + bim
                rre = pltpu.roll(are, nb, 0)
                rim = pltpu.roll(aim, nb, 0)
                sre = lr * rre - li * rim + bre
                sim = lr * rim + li * rre + bim
                s_ref[j, pl.ds(r0, SUBLANES), re_cols] = jnp.where(first_step, are, sre)
                s_ref[j, pl.ds(r0, SUBLANES), im_cols] = jnp.where(first_step, aim, sim)
                return pltpu.roll(sre, nb, 0), pltpu.roll(sim, nb, 0)

            prev0 = (carry_ref[j, :, re_cols], carry_ref[j, :, im_cols])
            pre, pim = lax.fori_loop(0, rows // SUBLANES, scan_tile, prev0)
            carry_ref[j, :, re_cols] = pre
            carry_ref[j, :, im_cols] = pim

    ys = [jnp.dot(s_ref[j].astype(BF16), cband_ref[j], preferred_element_type=F32) for j in range(2)]
    y = jnp.concatenate(ys, axis=-1) + svec_ref[0:1, :] * u
    z = _gelu_tanh(y)
    z = z * _sigmoid(jnp.dot(z.astype(BF16), wglu_ref[...], preferred_element_type=F32) + svec_ref[1:2, :])
    mrg_ref[:, D_CONV:] = _rms_norm(z, svec_ref[2:3, :]).astype(BF16)

    mix = jnp.dot(mrg_ref[...], wout_ref[...], preferred_element_type=F32)
    o_ref[...] = _layer_norm(alpha * x + mix, lnvec_ref[0:1, :], lnvec_ref[1:2, :])


def _ssm_bands(log_dt, a_re, a_im, b_re, b_im, c_re, c_im):
    g, p, hh = N_SSM_GROUPS, SSM_STATE, SSM_GROUP
    dt = jnp.exp(log_dt.astype(F32))[:, None]
    mag = jnp.exp(a_re * dt)
    ang = a_im * dt
    lb_re = mag * jnp.cos(ang)
    lb_im = mag * jnp.sin(ang)
    den = a_re * a_re + a_im * a_im
    coef_re = ((lb_re - 1.0) * a_re + lb_im * a_im) / den
    coef_im = (lb_im * a_re - (lb_re - 1.0) * a_im) / den
    bb_re = coef_re[..., None] * b_re - coef_im[..., None] * b_im
    bb_im = coef_re[..., None] * b_im + coef_im[..., None] * b_re
    eye = jnp.eye(g // 2, dtype=F32)

    def in_band(bb, j):
        blk = bb[j * (g // 2):(j + 1) * (g // 2)]
        return jnp.einsum('gph,gk->ghkp', blk, eye).reshape(g // 2 * hh, g // 2 * p)

    def out_band(cc, j):
        blk = cc[j * (g // 2):(j + 1) * (g // 2)]
        return jnp.einsum('ghp,gk->gpkh', blk, eye).reshape(g // 2 * p, g // 2 * hh)

    bband = jnp.stack([jnp.concatenate([in_band(bb_re, j), in_band(bb_im, j)], axis=1) for j in range(2)])
    cband = jnp.stack([jnp.concatenate([out_band(c_re, j), -out_band(c_im, j)], axis=0) for j in range(2)])
    lam = jnp.concatenate([lb_re.reshape(2, -1), lb_im.reshape(2, -1)], axis=0)
    return bband.astype(BF16), cband.astype(BF16), lam


def _mixer(x_rows, nb, alpha, w_in, conv_w, conv_b, conv_ln_g, conv_ln_b, log_dt, a_re, a_im, b_re, b_im,
           c_re, c_im, d_skip, w_glu, b_glu, g_conv_out, g_ssm_out, w_out, ln_g, ln_b):
    n_rows, d = x_rows.shape
    steps = n_rows // nb
    tile_steps = min(MIX_STEPS, steps)
    rows = tile_steps * nb
    assert nb * 2 == SUBLANES and steps % tile_steps == 0 and tile_steps >= HALO_STEPS
    assert rows % CONV_CHUNK == 0
    halo = HALO_STEPS * nb
    bband, cband, lam = _ssm_bands(log_dt, a_re, a_im, b_re, b_im, c_re, c_im)
    cvec = jnp.stack([conv_b, conv_ln_g, conv_ln_b, g_conv_out])
    svec = jnp.stack([d_skip, b_glu, g_ssm_out])
    lnvec = jnp.stack([ln_g, ln_b])
    operands = (x_rows, w_in.astype(BF16), conv_w, cvec, bband, lam, cband, svec,
                w_glu.astype(BF16), w_out.astype(BF16), lnvec)
    in_specs = [pl.BlockSpec((rows, d), lambda i: (i, 0))] + [_full_spec(a.shape) for a in operands[1:]]
    return pl.pallas_call(
        functools.partial(_mixer_kernel, nb=nb, alpha=alpha),
        out_shape=jax.ShapeDtypeStruct((n_rows, d), F32),
        grid=(n_rows // rows,),
        in_specs=in_specs,
        out_specs=pl.BlockSpec((rows, d), lambda i: (i, 0)),
        scratch_shapes=[
            pltpu.VMEM((halo + rows, D_CONV), F32),
            pltpu.VMEM((halo + SUBLANES + rows, D_CONV), F32),
            pltpu.VMEM((2, rows, N_STATE), F32),
            pltpu.VMEM((2, SUBLANES, N_STATE), F32),
            pltpu.VMEM((rows, D_CONV + D_SSM), BF16),
        ],
        compiler_params=pltpu.CompilerParams(dimension_semantics=("arbitrary",),
                                             vmem_limit_bytes=VMEM_LIMIT),
        name="mixer",
    )(*operands)


def _kv_kernel(mem_ref, wk_ref, wv_ref, k_ref, v_ref):
    m = mem_ref[0].astype(BF16)
    k_ref[0] = jnp.dot(m, wk_ref[...], preferred_element_type=F32).astype(BF16)
    v_ref[0] = jnp.dot(m, wv_ref[...], preferred_element_type=F32).astype(BF16)


def _kv_proj(mem, wk, wv):
    b, m, d = mem.shape
    blk = pl.BlockSpec((1, m, d), lambda i: (i, 0, 0))
    return pl.pallas_call(
        _kv_kernel,
        out_shape=(jax.ShapeDtypeStruct((b, m, d), BF16), jax.ShapeDtypeStruct((b, m, d), BF16)),
        grid=(b,),
        in_specs=[blk, _full_spec(wk.shape), _full_spec(wv.shape)],
        out_specs=(blk, blk),
        compiler_params=pltpu.CompilerParams(dimension_semantics=("arbitrary",),
                                             vmem_limit_bytes=VMEM_LIMIT),
        name="kv_proj",
    )(mem, wk.astype(BF16), wv.astype(BF16))


def _route(x2, wr_ref, rb_ref):
    t = x2.shape[0]
    logits = lax.dot_general(wr_ref[...], x2.astype(BF16), (((1,), (1,)), ((), ())),
                             preferred_element_type=F32)
    scores = _sigmoid(logits)
    biased = scores + rb_ref[...]
    neg = -jnp.inf

    in_group = lax.broadcasted_iota(jnp.int32, (GROUP_SIZE, t), 0).astype(F32)
    slabs, group_rows = [], []
    for g in range(N_EXPERT_GROUPS):
        slab = biased[g * GROUP_SIZE:(g + 1) * GROUP_SIZE, :]
        m1 = jnp.max(slab, axis=0, keepdims=True)
        i1 = jnp.min(jnp.where(slab == m1, in_group, float(GROUP_SIZE)), axis=0, keepdims=True)
        m2 = jnp.max(jnp.where(in_group == i1, neg, slab), axis=0, keepdims=True)
        slabs.append(slab)
        group_rows.append(m1 + m2)
    group_score = jnp.concatenate(group_rows, axis=0)

    gid = lax.broadcasted_iota(jnp.int32, (N_EXPERT_GROUPS, t), 0).astype(F32)
    chosen = jnp.zeros((N_EXPERT_GROUPS, t), F32)
    for _ in range(TOPK_GROUPS):
        m = jnp.max(group_score, axis=0, keepdims=True)
        gi = jnp.min(jnp.where(group_score == m, gid, float(N_EXPERT_GROUPS)), axis=0, keepdims=True)
        hit = gid == gi
        chosen = jnp.where(hit, 1.0, chosen)
        group_score = jnp.where(hit, neg, group_score)

    masked = jnp.concatenate(
        [jnp.where(chosen[g:g + 1, :] > 0.5, slabs[g], neg) for g in range(N_EXPERT_GROUPS)], axis=0)

    eid = lax.broadcasted_iota(jnp.int32, (N_EXPERTS, t), 0).astype(F32)
    ids, gates = [], []
    for _ in range(TOP_K):
        m = jnp.max(masked, axis=0, keepdims=True)
        ei = jnp.min(jnp.where(masked == m, eid, float(N_EXPERTS)), axis=0, keepdims=True)
        hit = eid == ei
        gates.append(jnp.sum(jnp.where(hit, scores, 0.0), axis=0, keepdims=True))
        ids.append(ei)
        masked = jnp.where(hit, neg, masked)
    ids = jnp.concatenate(ids, axis=0).astype(jnp.int32)
    gates = jnp.concatenate(gates, axis=0)
    gates = ROUTED_SCALE * gates / jnp.sum(gates, axis=0, keepdims=True)
    return ids, gates


def _attn_kernel(x_ref, k_ref, v_ref, wq_ref, wo_ref, lnvec_ref, wr_ref, rb_ref,
                 x2_ref, xt_ref, ids_ref, gates_ref, *, alpha):
    x = x_ref[0]
    q = jnp.dot(x.astype(BF16), wq_ref[...], preferred_element_type=F32)
    heads = []
    for h in range(N_XHEADS):
        cols = slice(h * XHEAD_DIM, (h + 1) * XHEAD_DIM)
        s = lax.dot_general(q[:, cols].astype(BF16), k_ref[0, :, cols], (((1,), (1,)), ((), ())),
                            preferred_element_type=F32) * (XHEAD_DIM ** -0.5)
        p = jnp.exp(s - jnp.max(s, axis=-1, keepdims=True))
        p = p / jnp.sum(p, axis=-1, keepdims=True)
        heads.append(jnp.dot(p.astype(BF16), v_ref[0, :, cols], preferred_element_type=F32))
    o = jnp.concatenate(heads, axis=-1)
    xa = jnp.dot(o.astype(BF16), wo_ref[...], preferred_element_type=F32)
    x2 = _layer_norm(alpha * x + xa, lnvec_ref[0:1, :], lnvec_ref[1:2, :])
    x2_ref[0] = x2
    for s in range(TOKEN_TILE_ROWS):
        xt_ref[0, pl.ds(s, x2.shape[0], stride=TOKEN_TILE_ROWS), :] = x2[:, s * LANES:(s + 1) * LANES]
    ids, gates = _route(x2, wr_ref, rb_ref)
    ids_ref[0] = ids
    gates_ref[0] = gates


def _attention_router(x, k, v, wq, wo, ln_g, ln_b, w_router, router_bias, alpha):
    b, l, d = x.shape
    m = k.shape[1]
    tq = min(ATTN_ROWS, l)
    assert l % tq == 0
    lnvec = jnp.stack([ln_g, ln_b])
    wr_t = w_router.T.astype(BF16)
    rb = router_bias.astype(F32)[:, None]
    assert d == TOKEN_TILE_ROWS * LANES
    tile = pl.BlockSpec((1, tq, d), lambda i, j: (i, j, 0))
    tok_tile = pl.BlockSpec((1, tq * TOKEN_TILE_ROWS, LANES), lambda i, j: (i, j, 0))
    mem_blk = pl.BlockSpec((1, m, d), lambda i, j: (i, 0, 0))
    topk_blk = pl.BlockSpec((1, TOP_K, tq), lambda i, j: (i, 0, j))
    return pl.pallas_call(
        functools.partial(_attn_kernel, alpha=alpha),
        out_shape=(jax.ShapeDtypeStruct((b, l, d), F32),
                   jax.ShapeDtypeStruct((b, l * TOKEN_TILE_ROWS, LANES), F32),
                   jax.ShapeDtypeStruct((b, TOP_K, l), jnp.int32),
                   jax.ShapeDtypeStruct((b, TOP_K, l), F32)),
        grid=(b, l // tq),
        in_specs=[tile, mem_blk, mem_blk, _full_spec(wq.shape), _full_spec(wo.shape),
                  _full_spec(lnvec.shape), _full_spec(wr_t.shape), _full_spec(rb.shape)],
        out_specs=(tile, tok_tile, topk_blk, topk_blk),
        compiler_params=pltpu.CompilerParams(dimension_semantics=("arbitrary", "arbitrary"),
                                             vmem_limit_bytes=VMEM_LIMIT),
        name="attention_router",
    )(x, k, v, wq.astype(BF16), wo.astype(BF16), lnvec, wr_t, rb)


def _moe_kernel(starts_ref, counts_ref, dest_hbm, gates_hbm, x_hbm, wg_ref, wu_ref, wd_ref, out_hbm,
                xs_ref, acc_ref, lhs_ref, y_ref, wgu_ref, wdn_ref, dest_ref, gsrc_ref, tok_ref, gate_ref, sem):
    part = pl.program_id(0)
    e = pl.program_id(1)
    n_e = pl.num_programs(1)
    tr = TOKEN_TILE_ROWS
    n_rows = out_hbm.shape[1]
    n_tok = n_rows // tr
    n_list = tok_ref.shape[1] - LANES

    def token_rows(t):
        return pl.ds(pl.multiple_of(t * tr, tr), tr)

    @pl.when(e == 0)
    def _():
        load = pltpu.make_async_copy(x_hbm.at[part], xs_ref.at[pl.ds(0, n_rows)], sem.at[0])
        load_dest = pltpu.make_async_copy(dest_hbm.at[part], dest_ref, sem.at[2])
        load_gates = pltpu.make_async_copy(gates_hbm.at[part], gsrc_ref, sem.at[3])
        load.start()
        load_dest.start()
        load_gates.start()
        acc_ref[...] = jnp.zeros_like(acc_ref)
        lhs_ref[...] = jnp.zeros_like(lhs_ref)
        xs_ref[pl.ds(n_rows, tr), :] = jnp.zeros((tr, LANES), F32)

        def fill_rounding(ex, c):
            first = starts_ref[part * n_e + ex] + counts_ref[part * n_e + ex]
            n_fill = (SUBLANES - first % SUBLANES) % SUBLANES
            for j in range(SUBLANES - 1):
                pos = jnp.where(j < n_fill, first + j, n_list + j)
                tok_ref[0, pos] = n_tok
                gate_ref[0, pos] = 0.0
            return c

        lax.fori_loop(0, n_e, fill_rounding, 0)
        load_dest.wait()
        load_gates.wait()

        def place(t, c):
            for k in range(TOP_K):
                pos = dest_ref[k, t]
                tok_ref[0, pos] = t
                gate_ref[0, pos] = gsrc_ref[k, t]
            return c

        lax.fori_loop(0, n_tok, place, 0)
        load.wait()

    start = starts_ref[part * n_e + e]
    count = counts_ref[part * n_e + e]

    def run_block(i0, n_groups, size):
        def gather(g, c):
            base = pl.multiple_of(g * SUBLANES, SUBLANES)
            for r in range(SUBLANES):
                t = tok_ref[0, i0 + base + r]
                lhs_ref[token_rows(base + r), :] = xs_ref[token_rows(t), :]
            return c

        lax.fori_loop(0, n_groups, gather, 0)
        xb = jnp.concatenate([lhs_ref[pl.ds(s, size, stride=tr), :] for s in range(tr)], axis=-1)
        gu = jnp.dot(xb.astype(BF16), wgu_ref[...], preferred_element_type=F32)
        hid = _silu(gu[:, :D_EXPERT]) * gu[:, D_EXPERT:]
        y = jnp.dot(hid.astype(BF16), wdn_ref[...], preferred_element_type=F32)
        for s in range(tr):
            y_ref[pl.ds(s, size, stride=tr), :] = y[:, s * LANES:(s + 1) * LANES]

        def scatter(g, c):
            base = pl.multiple_of(g * SUBLANES, SUBLANES)
            rows, vals = [], []
            for r in range(SUBLANES):
                t = tok_ref[0, i0 + base + r]
                rows.append(token_rows(t))
                vals.append(acc_ref[rows[-1], :] + gate_ref[0, i0 + base + r] * y_ref[token_rows(base + r), :])
            for row, val in zip(rows, vals):
                acc_ref[row, :] = val
            return c

        lax.fori_loop(0, n_groups, scatter, 0)

    @pl.when(count > 0)
    def _():
        wgu_ref[:, :D_EXPERT] = wg_ref[0].astype(BF16)
        wgu_ref[:, D_EXPERT:] = wu_ref[0].astype(BF16)
        wdn_ref[...] = wd_ref[0].astype(BF16)
        n_full = count // MOE_ROWS

        def full_block(bi, c):
            run_block(start + bi * MOE_ROWS, MOE_ROWS // SUBLANES, MOE_ROWS)
            return c

        lax.fori_loop(0, n_full, full_block, 0)
        rest = count - n_full * MOE_ROWS
        rest_groups = (rest + SUBLANES - 1) // SUBLANES

        @pl.when(rest > MOE_ROWS // 2)
        def _():
            run_block(start + n_full * MOE_ROWS, rest_groups, MOE_ROWS)

        @pl.when((rest > 0) & (rest <= MOE_ROWS // 2))
        def _():
            run_block(start + n_full * MOE_ROWS, rest_groups, MOE_ROWS // 2)

    @pl.when(e == n_e - 1)
    def _():
        store = pltpu.make_async_copy(acc_ref.at[pl.ds(0, n_rows)], out_hbm.at[part], sem.at[1])
        store.start()
        store.wait()


def _routed_experts(xt, dest, gates, starts, counts, w_gate, w_up, w_down):
    p, n_rows, lanes = xt.shape
    d = w_gate.shape[1]
    k, l = dest.shape[1:]
    n_list = l * k + N_EXPERTS * SUBLANES
    grid_spec = pltpu.PrefetchScalarGridSpec(
        num_scalar_prefetch=2,
        grid=(p, N_EXPERTS),
        in_specs=[pl.BlockSpec(memory_space=pl.ANY),
                  pl.BlockSpec(memory_space=pl.ANY),
                  pl.BlockSpec(memory_space=pl.ANY),
                  pl.BlockSpec((1, d, D_EXPERT), lambda i, e, *_: (e, 0, 0)),
                  pl.BlockSpec((1, d, D_EXPERT), lambda i, e, *_: (e, 0, 0)),
                  pl.BlockSpec((1, D_EXPERT, d), lambda i, e, *_: (e, 0, 0))],
        out_specs=pl.BlockSpec(memory_space=pl.ANY),
        scratch_shapes=[
            pltpu.VMEM((n_rows + TOKEN_TILE_ROWS, lanes), F32),
            pltpu.VMEM((n_rows + TOKEN_TILE_ROWS, lanes), F32),
            pltpu.VMEM((MOE_ROWS * TOKEN_TILE_ROWS, lanes), F32),
            pltpu.VMEM((MOE_ROWS * TOKEN_TILE_ROWS, lanes), F32),
            pltpu.VMEM((d, 2 * D_EXPERT), BF16),
            pltpu.VMEM((D_EXPERT, d), BF16),
            pltpu.SMEM((k, l), jnp.int32),
            pltpu.SMEM((k, l), F32),
            pltpu.SMEM((1, n_list + LANES), jnp.int32),
            pltpu.SMEM((1, n_list + LANES), F32),
            pltpu.SemaphoreType.DMA((4,)),
        ])
    return pl.pallas_call(
        _moe_kernel,
        out_shape=jax.ShapeDtypeStruct((p, n_rows, lanes), F32),
        grid_spec=grid_spec,
        compiler_params=pltpu.CompilerParams(dimension_semantics=("arbitrary", "arbitrary"),
                                             vmem_limit_bytes=VMEM_LIMIT),
        name="routed_experts",
    )(starts, counts, dest, gates, xt, w_gate, w_up, w_down)


def _dispatch_kernel(ids_ref, dest_ref, starts_ref, counts_ref, cnt_ref, *, chunk):
    n_tok = ids_ref.shape[2]
    n_chunks = n_tok // chunk
    eid = lax.broadcasted_iota(jnp.int32, (N_EXPERTS, chunk), 0)

    def one_hots(c):
        cols = pl.ds(pl.multiple_of(c * chunk, chunk), chunk)
        return [jnp.where(eid == ids_ref[0, k:k + 1, cols], 1.0, 0.0) for k in range(TOP_K)], cols

    def count_chunk(c, carry):
        hots, _ = one_hots(c)
        cnt_ref[...] += jnp.sum(sum(hots), axis=1, keepdims=True)
        return carry

    cnt_ref[...] = jnp.zeros_like(cnt_ref)
    lax.fori_loop(0, n_chunks, count_chunk, 0)
    counts = cnt_ref[...]
    counts_ref[0] = counts.astype(jnp.int32)

    rounded = jnp.floor((counts + (SUBLANES - 1)) * (1.0 / SUBLANES)) * SUBLANES
    row = lax.broadcasted_iota(jnp.int32, (N_EXPERTS, LANES), 0)
    scan = jnp.broadcast_to(rounded, (N_EXPERTS, LANES))
    shift = 1
    while shift < N_EXPERTS:
        scan = scan + jnp.where(row >= shift, pltpu.roll(scan, shift, 0), 0.0)
        shift *= 2
    starts = scan[:, 0:1] - rounded
    starts_ref[0] = starts.astype(jnp.int32)

    upper = jnp.where(lax.broadcasted_iota(jnp.int32, (chunk, chunk), 0)
                      < lax.broadcasted_iota(jnp.int32, (chunk, chunk), 1), 1.0, 0.0).astype(BF16)

    def place_chunk(c, seen):
        hots, cols = one_hots(c)
        mask = sum(hots)
        before = jnp.dot(mask.astype(BF16), upper, preferred_element_type=F32)
        pos = starts + seen + before
        for k in range(TOP_K):
            dest_ref[0, k:k + 1, cols] = jnp.sum(hots[k] * pos, axis=0, keepdims=True).astype(jnp.int32)
        return seen + jnp.sum(mask, axis=1, keepdims=True)

    lax.fori_loop(0, n_chunks, place_chunk, jnp.zeros((N_EXPERTS, 1), F32))


def _dispatch(ids):
    p, k, l = ids.shape
    chunk = min(256, l)
    assert l % chunk == 0
    ids_blk = pl.BlockSpec((1, k, l), lambda i: (i, 0, 0))
    col_blk = pl.BlockSpec((1, N_EXPERTS, 1), lambda i: (i, 0, 0))
    dest, starts, counts = pl.pallas_call(
        functools.partial(_dispatch_kernel, chunk=chunk),
        out_shape=(jax.ShapeDtypeStruct((p, k, l), jnp.int32),
                   jax.ShapeDtypeStruct((p, N_EXPERTS, 1), jnp.int32),
                   jax.ShapeDtypeStruct((p, N_EXPERTS, 1), jnp.int32)),
        grid=(p,),
        in_specs=[ids_blk],
        out_specs=(ids_blk, col_blk, col_blk),
        scratch_shapes=[pltpu.VMEM((N_EXPERTS, 1), F32)],
        compiler_params=pltpu.CompilerParams(dimension_semantics=("arbitrary",),
                                             vmem_limit_bytes=VMEM_LIMIT),
        name="dispatch",
    )(ids)
    return dest, starts.reshape(-1), counts.reshape(-1)


def _ffn_out_kernel(x_ref, routed_ref, wgu_ref, wdn_ref, lnvec_ref, o_ref, *, alpha):
    x = x_ref[...]
    rows = x.shape[0]
    routed = jnp.concatenate([routed_ref[pl.ds(s, rows, stride=TOKEN_TILE_ROWS), :]
                              for s in range(TOKEN_TILE_ROWS)], axis=-1)
    gu = jnp.dot(x.astype(BF16), wgu_ref[...], preferred_element_type=F32)
    d_sh = wdn_ref.shape[0]
    hid = _silu(gu[:, :d_sh]) * gu[:, d_sh:]
    shared = jnp.dot(hid.astype(BF16), wdn_ref[...], preferred_element_type=F32)
    o_ref[...] = _layer_norm(alpha * x + routed + shared, lnvec_ref[0:1, :], lnvec_ref[1:2, :])


def _ffn_out(x, routed_tiles, sh_gate, sh_up, sh_down, ln_g, ln_b, alpha):
    n, d = x.shape
    rows = min(FFN_ROWS, n)
    assert n % rows == 0
    wgu = jnp.concatenate([sh_gate, sh_up], axis=1).astype(BF16)
    lnvec = jnp.stack([ln_g, ln_b])
    tile = pl.BlockSpec((rows, d), lambda i: (i, 0))
    tok_tile = pl.BlockSpec((rows * TOKEN_TILE_ROWS, LANES), lambda i: (i, 0))
    return pl.pallas_call(
        functools.partial(_ffn_out_kernel, alpha=alpha),
        out_shape=jax.ShapeDtypeStruct((n, d), F32),
        grid=(n // rows,),
        in_specs=[tile, tok_tile, _full_spec(wgu.shape), _full_spec(sh_down.shape), _full_spec(lnvec.shape)],
        out_specs=tile,
        compiler_params=pltpu.CompilerParams(dimension_semantics=("arbitrary",),
                                             vmem_limit_bytes=VMEM_LIMIT),
        name="shared_ffn_norm",
    )(x, routed_tiles, wgu, sh_down.astype(BF16), lnvec)


def kernel(x, mem, w_in, conv_w, conv_b, conv_ln_g, conv_ln_b, ssm_log_dt, ssm_a_re, ssm_a_im, ssm_b_re, ssm_b_im, ssm_c_re, ssm_c_im, ssm_d, w_glu, b_glu, g_conv_out, g_ssm_out, w_out, ln1_g, ln1_b, wq, wk, wv, wo, ln2_g, ln2_b, w_router, router_bias, w_gate, w_up, w_down, sh_gate, sh_up, sh_down, ln3_g, ln3_b):
    depth = w_in.shape[0]
    alpha = (2 * depth) ** 0.25
    b, l, d = x.shape
    for i in range(depth):
        x_rows = jnp.swapaxes(x, 0, 1).reshape(l * b, d)
        x_rows = _mixer(x_rows, b, alpha, w_in[i], conv_w[i], conv_b[i], conv_ln_g[i], conv_ln_b[i],
                        ssm_log_dt[i], ssm_a_re[i], ssm_a_im[i], ssm_b_re[i], ssm_b_im[i],
                        ssm_c_re[i], ssm_c_im[i], ssm_d[i], w_glu[i], b_glu[i],
                        g_conv_out[i], g_ssm_out[i], w_out[i], ln1_g[i], ln1_b[i])
        x = jnp.swapaxes(x_rows.reshape(l, b, d), 0, 1)
        k, v = _kv_proj(mem, wk[i], wv[i])
        x, xt, ids, gates = _attention_router(x, k, v, wq[i], wo[i], ln2_g[i], ln2_b[i],
                                              w_router[i], router_bias[i], alpha)
        dest, starts, counts = _dispatch(ids)
        routed = _routed_experts(xt, dest, gates, starts, counts, w_gate[i], w_up[i], w_down[i])
        x = _ffn_out(x.reshape(b * l, d), routed.reshape(b * l * TOKEN_TILE_ROWS, LANES),
                     sh_gate[i], sh_up[i], sh_down[i], ln3_g[i], ln3_b[i], alpha).reshape(b, l, d)
    return x
```

```python
import functools

import jax
import jax.numpy as jnp
from jax import lax
from jax.experimental import pallas as pl
from jax.experimental.pallas import tpu as pltpu

F32 = jnp.float32
BF16 = jnp.bfloat16

D_MODEL = 1024
D_CONV = 512
D_SSM = 512
CONV_WIDTH = 31
SSM_GROUP = 16
N_SSM_GROUPS = D_SSM // SSM_GROUP
SSM_STATE = 64
N_STATE = N_SSM_GROUPS * SSM_STATE
N_XHEADS = 4
XHEAD_DIM = D_MODEL // N_XHEADS
N_EXPERTS = 256
TOP_K = 8
N_EXPERT_GROUPS = 8
GROUP_SIZE = N_EXPERTS // N_EXPERT_GROUPS
TOPK_GROUPS = 4
D_EXPERT = 256
ROUTED_SCALE = 2.5
EPS = 1e-5

SUBLANES = 8
LANES = 128
TOKEN_TILE_ROWS = D_MODEL // LANES
HALO_STEPS = 32
MIX_STEPS = 128
CONV_CHUNK = 64
SCAN_LANES = 512
ATTN_ROWS = 512
MOE_ROWS = 256
FFN_ROWS = 512
VMEM_LIMIT = 56 * 1024 * 1024


def _sigmoid(x):
    return 1.0 / (1.0 + jnp.exp(-x))


def _silu(x):
    return x * _sigmoid(x)


def _gelu_tanh(x):
    c = 0.7978845608028654
    return 0.5 * x * (1.0 + jnp.tanh(c * (x + 0.044715 * (x * x * x))))


def _layer_norm(x, g, b):
    mu = jnp.mean(x, -1, keepdims=True)
    xc = x - mu
    var = jnp.mean(xc * xc, -1, keepdims=True)
    return xc * lax.rsqrt(var + EPS) * g + b


def _rms_norm(x, g):
    return x * lax.rsqrt(jnp.mean(x * x, -1, keepdims=True) + EPS) * g


def _full_spec(shape):
    return pl.BlockSpec(shape, lambda *_: (0,) * len(shape))


def _mixer_kernel(x_ref, w_in_ref, cw_ref, cvec_ref, bband_ref, lam_ref, cband_ref, svec_ref,
                  wglu_ref, wout_ref, lnvec_ref, o_ref,
                  ha_ref, hb_ref, s_ref, carry_ref, mrg_ref, yc_ref, *, nb, alpha):
    rows = x_ref.shape[0]
    halo = HALO_STEPS * nb
    half = N_STATE // 2

    @pl.when(pl.program_id(0) == 0)
    def _():
        ha_ref[...] = jnp.zeros_like(ha_ref)
        hb_ref[...] = jnp.zeros_like(hb_ref)
        carry_ref[...] = jnp.zeros_like(carry_ref)

    x = x_ref[...]
    proj = jnp.dot(x.astype(BF16), w_in_ref[...], preferred_element_type=F32)

    h = proj[:, :D_CONV] * _sigmoid(proj[:, D_CONV:2 * D_CONV])
    ha_ref[halo:halo + rows, :] = h
    hb_ref[halo + nb:halo + nb + rows, :] = h

    tiles = CONV_CHUNK // SUBLANES
    for lb in range(D_CONV // LANES):
        cols = slice(lb * LANES, (lb + 1) * LANES)
        taps = [jnp.broadcast_to(cw_ref[k:k + 1, cols], (SUBLANES, LANES)) for k in range(CONV_WIDTH)]
        bias = jnp.broadcast_to(cvec_ref[0:1, cols], (SUBLANES, LANES))

        def conv_chunk(c, carry, cols=cols, taps=taps, bias=bias):
            r0 = pl.multiple_of(c * CONV_CHUNK, CONV_CHUNK)
            accs = [bias] * tiles
            for k in range(CONV_WIDTH):
                off = halo + (k - (CONV_WIDTH - 1)) * nb
                src, off = (ha_ref, off) if off % SUBLANES == 0 else (hb_ref, off + nb)
                for i in range(tiles):
                    accs[i] = accs[i] + taps[k] * src[pl.ds(r0 + off + i * SUBLANES, SUBLANES), cols]
            for i in range(tiles):
                yc_ref[pl.ds(r0 + i * SUBLANES, SUBLANES), cols] = accs[i]
            return carry

        lax.fori_loop(0, rows // CONV_CHUNK, conv_chunk, 0)
    ha_ref[0:halo, :] = ha_ref[rows:rows + halo, :]
    hb_ref[0:halo + SUBLANES, :] = hb_ref[rows:rows + halo + SUBLANES, :]
    yc = _silu(_layer_norm(yc_ref[...], cvec_ref[1:2, :], cvec_ref[2:3, :]))
    mrg_ref[:, 0:D_CONV] = _rms_norm(yc, cvec_ref[3:4, :]).astype(BF16)

    u = proj[:, 2 * D_CONV:]
    ub = u.astype(BF16)
    for j in range(2):
        s_ref[j] = jnp.dot(ub[:, j * (D_SSM // 2):(j + 1) * (D_SSM // 2)], bband_ref[j],
                           preferred_element_type=F32)

    first_step = lax.broadcasted_iota(jnp.int32, (SUBLANES, SCAN_LANES), 0) < nb
    for j in range(2):
        for c in range(half // SCAN_LANES):
            re_cols = slice(c * SCAN_LANES, (c + 1) * SCAN_LANES)
            im_cols = slice(half + c * SCAN_LANES, half + (c + 1) * SCAN_LANES)
            lr = jnp.broadcast_to(lam_ref[j:j + 1, re_cols], (SUBLANES, SCAN_LANES))
            li = jnp.broadcast_to(lam_ref[2 + j:3 + j, re_cols], (SUBLANES, SCAN_LANES))

            def scan_tile(i, prev, j=j, re_cols=re_cols, im_cols=im_cols, lr=lr, li=li):
                pre, pim = prev
                r0 = pl.multiple_of(i * SUBLANES, SUBLANES)
                bre = s_ref[j, pl.ds(r0, SUBLANES), re_cols]
                bim = s_ref[j, pl.ds(r0, SUBLANES), im_cols]
                are = lr * pre - li * pim + bre
                aim = lr * pim + li * pre + bim
                rre = pltpu.roll(are, nb, 0)
                rim = pltpu.roll(aim, nb, 0)
                sre = lr * rre - li * rim + bre
                sim = lr * rim + li * rre + bim
                s_ref[j, pl.ds(r0, SUBLANES), re_cols] = jnp.where(first_step, are, sre)
                s_ref[j, pl.ds(r0, SUBLANES), im_cols] = jnp.where(first_step, aim, sim)
                return pltpu.roll(sre, nb, 0), pltpu.roll(sim, nb, 0)

            prev0 = (carry_ref[j, :, re_cols], carry_ref[j, :, im_cols])
            pre, pim = lax.fori_loop(0, rows // SUBLANES, scan_tile, prev0)
            carry_ref[j, :, re_cols] = pre
            carry_ref[j, :, im_cols] = pim

    ys = [jnp.dot(s_ref[j].astype(BF16), cband_ref[j], preferred_element_type=F32) for j in range(2)]
    y = jnp.concatenate(ys, axis=-1) + svec_ref[0:1, :] * u
    z = _gelu_tanh(y)
    z = z * _sigmoid(jnp.dot(z.astype(BF16), wglu_ref[...], preferred_element_type=F32) + svec_ref[1:2, :])
    mrg_ref[:, D_CONV:] = _rms_norm(z, svec_ref[2:3, :]).astype(BF16)

    mix = jnp.dot(mrg_ref[...], wout_ref[...], preferred_element_type=F32)
    o_ref[...] = _layer_norm(alpha * x + mix, lnvec_ref[0:1, :], lnvec_ref[1:2, :])


def _ssm_bands(log_dt, a_re, a_im, b_re, b_im, c_re, c_im):
    g, p, hh = N_SSM_GROUPS, SSM_STATE, SSM_GROUP
    dt = jnp.exp(log_dt.astype(F32))[:, None]
    mag = jnp.exp(a_re * dt)
    ang = a_im * dt
    lb_re = mag * jnp.cos(ang)
    lb_im = mag * jnp.sin(ang)
    den = a_re * a_re + a_im * a_im
    coef_re = ((lb_re - 1.0) * a_re + lb_im * a_im) / den
    coef_im = (lb_im * a_re - (lb_re - 1.0) * a_im) / den
    bb_re = coef_re[..., None] * b_re - coef_im[..., None] * b_im
    bb_im = coef_re[..., None] * b_im + coef_im[..., None] * b_re
    eye = jnp.eye(g // 2, dtype=F32)

    def in_band(bb, j):
        blk = bb[j * (g // 2):(j + 1) * (g // 2)]
        return jnp.einsum('gph,gk->ghkp', blk, eye).reshape(g // 2 * hh, g // 2 * p)

    def out_band(cc, j):
        blk = cc[j * (g // 2):(j + 1) * (g // 2)]
        return jnp.einsum('ghp,gk->gpkh', blk, eye).reshape(g // 2 * p, g // 2 * hh)

    bband = jnp.stack([jnp.concatenate([in_band(bb_re, j), in_band(bb_im, j)], axis=1) for j in range(2)])
    cband = jnp.stack([jnp.concatenate([out_band(c_re, j), -out_band(c_im, j)], axis=0) for j in range(2)])
    lam = jnp.concatenate([lb_re.reshape(2, -1), lb_im.reshape(2, -1)], axis=0)
    return bband.astype(BF16), cband.astype(BF16), lam


def _mixer(x_rows, nb, alpha, w_in, conv_w, conv_b, conv_ln_g, conv_ln_b, log_dt, a_re, a_im, b_re, b_im,
           c_re, c_im, d_skip, w_glu, b_glu, g_conv_out, g_ssm_out, w_out, ln_g, ln_b):
    n_rows, d = x_rows.shape
    steps = n_rows // nb
    tile_steps = min(MIX_STEPS, steps)
    rows = tile_steps * nb
    assert nb * 2 == SUBLANES and steps % tile_steps == 0 and tile_steps >= HALO_STEPS
    assert rows % CONV_CHUNK == 0
    halo = HALO_STEPS * nb
    bband, cband, lam = _ssm_bands(log_dt, a_re, a_im, b_re, b_im, c_re, c_im)
    cvec = jnp.stack([conv_b, conv_ln_g, conv_ln_b, g_conv_out])
    svec = jnp.stack([d_skip, b_glu, g_ssm_out])
    lnvec = jnp.stack([ln_g, ln_b])
    operands = (x_rows, w_in.astype(BF16), conv_w, cvec, bband, lam, cband, svec,
                w_glu.astype(BF16), w_out.astype(BF16), lnvec)
    in_specs = [pl.BlockSpec((rows, d), lambda i: (i, 0))] + [_full_spec(a.shape) for a in operands[1:]]
    return pl.pallas_call(
        functools.partial(_mixer_kernel, nb=nb, alpha=alpha),
        out_shape=jax.ShapeDtypeStruct((n_rows, d), F32),
        grid=(n_rows // rows,),
        in_specs=in_specs,
        out_specs=pl.BlockSpec((rows, d), lambda i: (i, 0)),
        scratch_shapes=[
            pltpu.VMEM((halo + rows, D_CONV), F32),
            pltpu.VMEM((halo + SUBLANES + rows, D_CONV), F32),
            pltpu.VMEM((2, rows, N_STATE), F32),
            pltpu.VMEM((2, SUBLANES, N_STATE), F32),
            pltpu.VMEM((rows, D_CONV + D_SSM), BF16),
            pltpu.VMEM((rows, D_CONV), F32),
        ],
        compiler_params=pltpu.CompilerParams(dimension_semantics=("arbitrary",),
                                             vmem_limit_bytes=VMEM_LIMIT),
        name="mixer",
    )(*operands)


def _kv_kernel(mem_ref, wk_ref, wv_ref, k_ref, v_ref):
    m = mem_ref[0].astype(BF16)
    k_ref[0] = jnp.dot(m, wk_ref[...], preferred_element_type=F32).astype(BF16)
    v_ref[0] = jnp.dot(m, wv_ref[...], preferred_element_type=F32).astype(BF16)


def _kv_proj(mem, wk, wv):
    b, m, d = mem.shape
    blk = pl.BlockSpec((1, m, d), lambda i: (i, 0, 0))
    return pl.pallas_call(
        _kv_kernel,
        out_shape=(jax.ShapeDtypeStruct((b, m, d), BF16), jax.ShapeDtypeStruct((b, m, d), BF16)),
        grid=(b,),
        in_specs=[blk, _full_spec(wk.shape), _full_spec(wv.shape)],
        out_specs=(blk, blk),
        compiler_params=pltpu.CompilerParams(dimension_semantics=("arbitrary",),
                                             vmem_limit_bytes=VMEM_LIMIT),
        name="kv_proj",
    )(mem, wk.astype(BF16), wv.astype(BF16))


def _route(x2, wr_ref, rb_ref):
    t = x2.shape[0]
    logits = lax.dot_general(wr_ref[...], x2.astype(BF16), (((1,), (1,)), ((), ())),
                             preferred_element_type=F32)
    scores = _sigmoid(logits)
    biased = scores + rb_ref[...]
    neg = -jnp.inf

    in_group = lax.broadcasted_iota(jnp.int32, (GROUP_SIZE, t), 0).astype(F32)
    slabs, group_rows = [], []
    for g in range(N_EXPERT_GROUPS):
        slab = biased[g * GROUP_SIZE:(g + 1) * GROUP_SIZE, :]
        m1 = jnp.max(slab, axis=0, keepdims=True)
        i1 = jnp.min(jnp.where(slab == m1, in_group, float(GROUP_SIZE)), axis=0, keepdims=True)
        m2 = jnp.max(jnp.where(in_group == i1, neg, slab), axis=0, keepdims=True)
        slabs.append(slab)
        group_rows.append(m1 + m2)
    group_score = jnp.concatenate(group_rows, axis=0)

    gid = lax.broadcasted_iota(jnp.int32, (N_EXPERT_GROUPS, t), 0).astype(F32)
    chosen = jnp.zeros((N_EXPERT_GROUPS, t), F32)
    for _ in range(TOPK_GROUPS):
        m = jnp.max(group_score, axis=0, keepdims=True)
        gi = jnp.min(jnp.where(group_score == m, gid, float(N_EXPERT_GROUPS)), axis=0, keepdims=True)
        hit = gid == gi
        chosen = jnp.where(hit, 1.0, chosen)
        group_score = jnp.where(hit, neg, group_score)

    masked = jnp.concatenate(
        [jnp.where(chosen[g:g + 1, :] > 0.5, slabs[g], neg) for g in range(N_EXPERT_GROUPS)], axis=0)

    eid = lax.broadcasted_iota(jnp.int32, (N_EXPERTS, t), 0).astype(F32)
    ids, gates = [], []
    for _ in range(TOP_K):
        m = jnp.max(masked, axis=0, keepdims=True)
        ei = jnp.min(jnp.where(masked == m, eid, float(N_EXPERTS)), axis=0, keepdims=True)
        hit = eid == ei
        gates.append(jnp.sum(jnp.where(hit, scores, 0.0), axis=0, keepdims=True))
        ids.append(ei)
        masked = jnp.where(hit, neg, masked)
    ids = jnp.concatenate(ids, axis=0).astype(jnp.int32)
    gates = jnp.concatenate(gates, axis=0)
    gates = ROUTED_SCALE * gates / jnp.sum(gates, axis=0, keepdims=True)
    return ids, gates


def _attn_kernel(x_ref, k_ref, v_ref, wq_ref, wo_ref, lnvec_ref, wr_ref, rb_ref,
                 x2_ref, xt_ref, ids_ref, gates_ref, *, alpha):
    x = x_ref[0]
    q = jnp.dot(x.astype(BF16), wq_ref[...], preferred_element_type=F32)
    heads = []
    for h in range(N_XHEADS):
        cols = slice(h * XHEAD_DIM, (h + 1) * XHEAD_DIM)
        s = lax.dot_general(q[:, cols].astype(BF16), k_ref[0, :, cols], (((1,), (1,)), ((), ())),
                            preferred_element_type=F32) * (XHEAD_DIM ** -0.5)
        p = jnp.exp(s - jnp.max(s, axis=-1, keepdims=True))
        p = p / jnp.sum(p, axis=-1, keepdims=True)
        heads.append(jnp.dot(p.astype(BF16), v_ref[0, :, cols], preferred_element_type=F32))
    o = jnp.concatenate(heads, axis=-1)
    xa = jnp.dot(o.astype(BF16), wo_ref[...], preferred_element_type=F32)
    x2 = _layer_norm(alpha * x + xa, lnvec_ref[0:1, :], lnvec_ref[1:2, :])
    x2_ref[0] = x2
    for s in range(TOKEN_TILE_ROWS):
        xt_ref[0, pl.ds(s, x2.shape[0], stride=TOKEN_TILE_ROWS), :] = x2[:, s * LANES:(s + 1) * LANES]
    ids, gates = _route(x2, wr_ref, rb_ref)
    ids_ref[0] = ids
    gates_ref[0] = gates


def _attention_router(x, k, v, wq, wo, ln_g, ln_b, w_router, router_bias, alpha):
    b, l, d = x.shape
    m = k.shape[1]
    tq = min(ATTN_ROWS, l)
    assert l % tq == 0
    lnvec = jnp.stack([ln_g, ln_b])
    wr_t = w_router.T.astype(BF16)
    rb = router_bias.astype(F32)[:, None]
    assert d == TOKEN_TILE_ROWS * LANES
    tile = pl.BlockSpec((1, tq, d), lambda i, j: (i, j, 0))
    tok_tile = pl.BlockSpec((1, tq * TOKEN_TILE_ROWS, LANES), lambda i, j: (i, j, 0))
    mem_blk = pl.BlockSpec((1, m, d), lambda i, j: (i, 0, 0))
    topk_blk = pl.BlockSpec((1, TOP_K, tq), lambda i, j: (i, 0, j))
    return pl.pallas_call(
        functools.partial(_attn_kernel, alpha=alpha),
        out_shape=(jax.ShapeDtypeStruct((b, l, d), F32),
                   jax.ShapeDtypeStruct((b, l * TOKEN_TILE_ROWS, LANES), F32),
                   jax.ShapeDtypeStruct((b, TOP_K, l), jnp.int32),
                   jax.ShapeDtypeStruct((b, TOP_K, l), F32)),
        grid=(b, l // tq),
        in_specs=[tile, mem_blk, mem_blk, _full_spec(wq.shape), _full_spec(wo.shape),
                  _full_spec(lnvec.shape), _full_spec(wr_t.shape), _full_spec(rb.shape)],
        out_specs=(tile, tok_tile, topk_blk, topk_blk),
        compiler_params=pltpu.CompilerParams(dimension_semantics=("arbitrary", "arbitrary"),
                                             vmem_limit_bytes=VMEM_LIMIT),
        name="attention_router",
    )(x, k, v, wq.astype(BF16), wo.astype(BF16), lnvec, wr_t, rb)


def _moe_kernel(starts_ref, counts_ref, dest_hbm, gates_hbm, x_hbm, wg_hbm, wu_hbm, wd_hbm, out_hbm,
                xs_ref, acc_ref, lhs_ref, y_ref, wg_buf, wu_buf, wd_buf, wgu_ref, wdn_ref,
                dest_ref, gsrc_ref, list_ref, sem, wsem):
    part = pl.program_id(0)
    n_e = wg_hbm.shape[0]
    tr = TOKEN_TILE_ROWS
    n_rows = out_hbm.shape[1]
    n_tok = n_rows // tr
    n_list = list_ref.shape[1] - LANES
    spare_code = n_tok * TOP_K

    def code_rows(code):
        token = lax.shift_right_logical(code, TOP_K.bit_length() - 1)
        return pl.ds(pl.multiple_of(token * tr, tr), tr)

    def tile_rows(i):
        return pl.ds(pl.multiple_of(i * tr, tr), tr)

    def weight_copies(e, slot):
        return (pltpu.make_async_copy(wg_hbm.at[e], wg_buf.at[slot], wsem.at[slot, 0]),
                pltpu.make_async_copy(wu_hbm.at[e], wu_buf.at[slot], wsem.at[slot, 1]),
                pltpu.make_async_copy(wd_hbm.at[e], wd_buf.at[slot], wsem.at[slot, 2]))

    for cp in weight_copies(0, 0):
        cp.start()
    load = pltpu.make_async_copy(x_hbm.at[part], xs_ref.at[pl.ds(0, n_rows)], sem.at[0])
    load_dest = pltpu.make_async_copy(dest_hbm.at[part], dest_ref, sem.at[2])
    load_gates = pltpu.make_async_copy(gates_hbm.at[part], gsrc_ref.at[:, pl.ds(0, spare_code)], sem.at[3])
    load.start()
    load_dest.start()
    load_gates.start()
    acc_ref[...] = jnp.zeros_like(acc_ref)
    lhs_ref[...] = jnp.zeros_like(lhs_ref)
    xs_ref[pl.ds(n_rows, tr), :] = jnp.zeros((tr, LANES), F32)
    gsrc_ref[0, spare_code] = 0.0

    def fill_rounding(ex, c):
        first = starts_ref[part * n_e + ex] + counts_ref[part * n_e + ex]
        n_fill = (SUBLANES - first % SUBLANES) % SUBLANES
        for j in range(SUBLANES - 1):
            list_ref[0, jnp.where(j < n_fill, first + j, n_list + j)] = spare_code
        return c

    lax.fori_loop(0, n_e, fill_rounding, 0)
    load_dest.wait()

    def place(t, c):
        for k in range(TOP_K):
            list_ref[0, dest_ref[k, t]] = t * TOP_K + k
        return c

    lax.fori_loop(0, n_tok, place, 0)
    load_gates.wait()
    load.wait()

    def run_block(i0, n_groups, size):
        def gather(g, c):
            base = pl.multiple_of(g * SUBLANES, SUBLANES)
            for r in range(SUBLANES):
                lhs_ref[tile_rows(base + r), :] = xs_ref[code_rows(list_ref[0, i0 + base + r]), :]
            return c

        lax.fori_loop(0, n_groups, gather, 0)
        xb = jnp.concatenate([lhs_ref[pl.ds(s, size, stride=tr), :] for s in range(tr)], axis=-1)
        gu = jnp.dot(xb.astype(BF16), wgu_ref[...], preferred_element_type=F32)
        hid = _silu(gu[:, :D_EXPERT]) * gu[:, D_EXPERT:]
        y = jnp.dot(hid.astype(BF16), wdn_ref[...], preferred_element_type=F32)
        for s in range(tr):
            y_ref[pl.ds(s, size, stride=tr), :] = y[:, s * LANES:(s + 1) * LANES]

        def scatter(g, c):
            base = pl.multiple_of(g * SUBLANES, SUBLANES)
            rows, vals = [], []
            for r in range(SUBLANES):
                code = list_ref[0, i0 + base + r]
                rows.append(code_rows(code))
                vals.append(acc_ref[rows[-1], :] + gsrc_ref[0, code] * y_ref[tile_rows(base + r), :])
            for row, val in zip(rows, vals):
                acc_ref[row, :] = val
            return c

        lax.fori_loop(0, n_groups, scatter, 0)

    def expert(e, c):
        slot = e % 2

        @pl.when(e + 1 < n_e)
        def _():
            for cp in weight_copies(e + 1, 1 - slot):
                cp.start()

        for cp in weight_copies(e, slot):
            cp.wait()
        start = starts_ref[part * n_e + e]
        count = counts_ref[part * n_e + e]

        @pl.when(count > 0)
        def _():
            wgu_ref[:, :D_EXPERT] = wg_buf[slot].astype(BF16)
            wgu_ref[:, D_EXPERT:] = wu_buf[slot].astype(BF16)
            wdn_ref[...] = wd_buf[slot].astype(BF16)
            n_full = count // MOE_ROWS

            def full_block(bi, cc):
                run_block(start + bi * MOE_ROWS, MOE_ROWS // SUBLANES, MOE_ROWS)
                return cc

            lax.fori_loop(0, n_full, full_block, 0)
            rest = count - n_full * MOE_ROWS
            rest_groups = (rest + SUBLANES - 1) // SUBLANES

            @pl.when(rest > MOE_ROWS // 2)
            def _():
                run_block(start + n_full * MOE_ROWS, rest_groups, MOE_ROWS)

            @pl.when((rest > 0) & (rest <= MOE_ROWS // 2))
            def _():
                run_block(start + n_full * MOE_ROWS, rest_groups, MOE_ROWS // 2)

        return c

    lax.fori_loop(0, n_e, expert, 0)
    store = pltpu.make_async_copy(acc_ref.at[pl.ds(0, n_rows)], out_hbm.at[part], sem.at[1])
    store.start()
    store.wait()


def _routed_experts(xt, dest, gates, starts, counts, w_gate, w_up, w_down):
    p, n_rows, lanes = xt.shape
    n_e, d, d_e = w_gate.shape
    k, l = dest.shape[1:]
    n_list = l * k + n_e * SUBLANES
    grid_spec = pltpu.PrefetchScalarGridSpec(
        num_scalar_prefetch=2,
        grid=(p,),
        in_specs=[pl.BlockSpec(memory_space=pl.ANY)] * 6,
        out_specs=pl.BlockSpec(memory_space=pl.ANY),
        scratch_shapes=[
            pltpu.VMEM((n_rows + TOKEN_TILE_ROWS, lanes), F32),
            pltpu.VMEM((n_rows + TOKEN_TILE_ROWS, lanes), F32),
            pltpu.VMEM((MOE_ROWS * TOKEN_TILE_ROWS, lanes), F32),
            pltpu.VMEM((MOE_ROWS * TOKEN_TILE_ROWS, lanes), F32),
            pltpu.VMEM((2, d, d_e), F32),
            pltpu.VMEM((2, d, d_e), F32),
            pltpu.VMEM((2, d_e, d), F32),
            pltpu.VMEM((d, 2 * d_e), BF16),
            pltpu.VMEM((d_e, d), BF16),
            pltpu.SMEM((k, l), jnp.int32),
            pltpu.SMEM((1, l * k + LANES), F32),
            pltpu.SMEM((1, n_list + LANES), jnp.int32),
            pltpu.SemaphoreType.DMA((4,)),
            pltpu.SemaphoreType.DMA((2, 3)),
        ])
    return pl.pallas_call(
        _moe_kernel,
        out_shape=jax.ShapeDtypeStruct((p, n_rows, lanes), F32),
        grid_spec=grid_spec,
        compiler_params=pltpu.CompilerParams(dimension_semantics=("arbitrary",),
                                             vmem_limit_bytes=VMEM_LIMIT),
        name="routed_experts",
    )(starts, counts, dest, gates, xt, w_gate, w_up, w_down)


def _dispatch_kernel(ids_ref, dest_ref, starts_ref, counts_ref, cnt_ref, *, chunk):
    n_tok = ids_ref.shape[2]
    n_chunks = n_tok // chunk
    eid = lax.broadcasted_iota(jnp.int32, (N_EXPERTS, chunk), 0)

    def one_hots(c):
        cols = pl.ds(pl.multiple_of(c * chunk, chunk), chunk)
        return [jnp.where(eid == ids_ref[0, k:k + 1, cols], 1.0, 0.0) for k in range(TOP_K)], cols

    def count_chunk(c, carry):
        hots, _ = one_hots(c)
        cnt_ref[...] += jnp.sum(sum(hots), axis=1, keepdims=True)
        return carry

    cnt_ref[...] = jnp.zeros_like(cnt_ref)
    lax.fori_loop(0, n_chunks, count_chunk, 0)
    counts = cnt_ref[...]
    counts_ref[0] = counts.astype(jnp.int32)

    rounded = jnp.floor((counts + (SUBLANES - 1)) * (1.0 / SUBLANES)) * SUBLANES
    row = lax.broadcasted_iota(jnp.int32, (N_EXPERTS, LANES), 0)
    scan = jnp.broadcast_to(rounded, (N_EXPERTS, LANES))
    shift = 1
    while shift < N_EXPERTS:
        scan = scan + jnp.where(row >= shift, pltpu.roll(scan, shift, 0), 0.0)
        shift *= 2
    starts = scan[:, 0:1] - rounded
    starts_ref[0] = starts.astype(jnp.int32)

    upper = jnp.where(lax.broadcasted_iota(jnp.int32, (chunk, chunk), 0)
                      < lax.broadcasted_iota(jnp.int32, (chunk, chunk), 1), 1.0, 0.0).astype(BF16)

    def place_chunk(c, seen):
        hots, cols = one_hots(c)
        mask = sum(hots)
        before = jnp.dot(mask.astype(BF16), upper, preferred_element_type=F32)
        pos = starts + seen + before
        for k in range(TOP_K):
            dest_ref[0, k:k + 1, cols] = jnp.sum(hots[k] * pos, axis=0, keepdims=True).astype(jnp.int32)
        return seen + jnp.sum(mask, axis=1, keepdims=True)

    lax.fori_loop(0, n_chunks, place_chunk, jnp.zeros((N_EXPERTS, 1), F32))


def _dispatch(ids):
    p, k, l = ids.shape
    chunk = min(256, l)
    assert l % chunk == 0
    ids_blk = pl.BlockSpec((1, k, l), lambda i: (i, 0, 0))
    col_blk = pl.BlockSpec((1, N_EXPERTS, 1), lambda i: (i, 0, 0))
    dest, starts, counts = pl.pallas_call(
        functools.partial(_dispatch_kernel, chunk=chunk),
        out_shape=(jax.ShapeDtypeStruct((p, k, l), jnp.int32),
                   jax.ShapeDtypeStruct((p, N_EXPERTS, 1), jnp.int32),
                   jax.ShapeDtypeStruct((p, N_EXPERTS, 1), jnp.int32)),
        grid=(p,),
        in_specs=[ids_blk],
        out_specs=(ids_blk, col_blk, col_blk),
        scratch_shapes=[pltpu.VMEM((N_EXPERTS, 1), F32)],
        compiler_params=pltpu.CompilerParams(dimension_semantics=("arbitrary",),
                                             vmem_limit_bytes=VMEM_LIMIT),
        name="dispatch",
    )(ids)
    return dest, starts.reshape(-1), counts.reshape(-1)


def _ffn_out_kernel(x_ref, routed_ref, wgu_ref, wdn_ref, lnvec_ref, o_ref, *, alpha):
    x = x_ref[...]
    rows = x.shape[0]
    routed = jnp.concatenate([routed_ref[pl.ds(s, rows, stride=TOKEN_TILE_ROWS), :]
                              for s in range(TOKEN_TILE_ROWS)], axis=-1)
    gu = jnp.dot(x.astype(BF16), wgu_ref[...], preferred_element_type=F32)
    d_sh = wdn_ref.shape[0]
    hid = _silu(gu[:, :d_sh]) * gu[:, d_sh:]
    shared = jnp.dot(hid.astype(BF16), wdn_ref[...], preferred_element_type=F32)
    o_ref[...] = _layer_norm(alpha * x + routed + shared, lnvec_ref[0:1, :], lnvec_ref[1:2, :])


def _ffn_out(x, routed_tiles, sh_gate, sh_up, sh_down, ln_g, ln_b, alpha):
    n, d = x.shape
    rows = min(FFN_ROWS, n)
    assert n % rows == 0
    wgu = jnp.concatenate([sh_gate, sh_up], axis=1).astype(BF16)
    lnvec = jnp.stack([ln_g, ln_b])
    tile = pl.BlockSpec((rows, d), lambda i: (i, 0))
    tok_tile = pl.BlockSpec((rows * TOKEN_TILE_ROWS, LANES), lambda i: (i, 0))
    return pl.pallas_call(
        functools.partial(_ffn_out_kernel, alpha=alpha),
        out_shape=jax.ShapeDtypeStruct((n, d), F32),
        grid=(n // rows,),
        in_specs=[tile, tok_tile, _full_spec(wgu.shape), _full_spec(sh_down.shape), _full_spec(lnvec.shape)],
        out_specs=tile,
        compiler_params=pltpu.CompilerParams(dimension_semantics=("arbitrary",),
                                             vmem_limit_bytes=VMEM_LIMIT),
        name="shared_ffn_norm",
    )(x, routed_tiles, wgu, sh_down.astype(BF16), lnvec)


def kernel(x, mem, w_in, conv_w, conv_b, conv_ln_g, conv_ln_b, ssm_log_dt, ssm_a_re, ssm_a_im, ssm_b_re, ssm_b_im, ssm_c_re, ssm_c_im, ssm_d, w_glu, b_glu, g_conv_out, g_ssm_out, w_out, ln1_g, ln1_b, wq, wk, wv, wo, ln2_g, ln2_b, w_router, router_bias, w_gate, w_up, w_down, sh_gate, sh_up, sh_down, ln3_g, ln3_b):
    depth = w_in.shape[0]
    alpha = (2 * depth) ** 0.25
    b, l, d = x.shape
    for i in range(depth):
        x_rows = jnp.swapaxes(x, 0, 1).reshape(l * b, d)
        x_rows = _mixer(x_rows, b, alpha, w_in[i], conv_w[i], conv_b[i], conv_ln_g[i], conv_ln_b[i],
                        ssm_log_dt[i], ssm_a_re[i], ssm_a_im[i], ssm_b_re[i], ssm_b_im[i],
                        ssm_c_re[i], ssm_c_im[i], ssm_d[i], w_glu[i], b_glu[i],
                        g_conv_out[i], g_ssm_out[i], w_out[i], ln1_g[i], ln1_b[i])
        x = jnp.swapaxes(x_rows.reshape(l, b, d), 0, 1)
        k, v = _kv_proj(mem, wk[i], wv[i])
        x, xt, ids, gates = _attention_router(x, k, v, wq[i], wo[i], ln2_g[i], ln2_b[i],
                                              w_router[i], router_bias[i], alpha)
        dest, starts, counts = _dispatch(ids)
        gates_tm = jnp.swapaxes(gates, 1, 2).reshape(b, 1, l * TOP_K)
        routed = _routed_experts(xt, dest, gates_tm, starts, counts, w_gate[i], w_up[i], w_down[i])
        x = _ffn_out(x.reshape(b * l, d), routed.reshape(b * l * TOKEN_TILE_ROWS, LANES),
                     sh_gate[i], sh_up[i], sh_down[i], ln3_g[i], ln3_b[i], alpha).reshape(b, l, d)
    return x
```

```python
import functools

import jax
import jax.numpy as jnp
from jax import lax
from jax.experimental import pallas as pl
from jax.experimental.pallas import tpu as pltpu

F32 = jnp.float32
BF16 = jnp.bfloat16

D_MODEL = 1024
D_CONV = 512
D_SSM = 512
CONV_WIDTH = 31
SSM_GROUP = 16
N_SSM_GROUPS = D_SSM // SSM_GROUP
SSM_STATE = 64
N_STATE = N_SSM_GROUPS * SSM_STATE
N_XHEADS = 4
XHEAD_DIM = D_MODEL // N_XHEADS
N_EXPERTS = 256
TOP_K = 8
N_EXPERT_GROUPS = 8
GROUP_SIZE = N_EXPERTS // N_EXPERT_GROUPS
TOPK_GROUPS = 4
D_EXPERT = 256
ROUTED_SCALE = 2.5
EPS = 1e-5

SUBLANES = 8
LANES = 128
TOKEN_TILE_ROWS = D_MODEL // LANES
HALO_STEPS = 32
MIX_STEPS = 128
CONV_CHUNK = 64
SCAN_LANES = 512
ATTN_ROWS = 512
MOE_ROWS = 256
FFN_ROWS = 512
WEIGHT_SLOTS = 3
WEIGHT_DMA_CHUNKS = 4
VMEM_LIMIT = 56 * 1024 * 1024


def _sigmoid(x):
    return 1.0 / (1.0 + jnp.exp(-x))


def _silu(x):
    return x * _sigmoid(x)


def _gelu_tanh(x):
    c = 0.7978845608028654
    return 0.5 * x * (1.0 + jnp.tanh(c * (x + 0.044715 * (x * x * x))))


def _layer_norm(x, g, b):
    mu = jnp.mean(x, -1, keepdims=True)
    xc = x - mu
    var = jnp.mean(xc * xc, -1, keepdims=True)
    return xc * lax.rsqrt(var + EPS) * g + b


def _rms_norm(x, g):
    return x * lax.rsqrt(jnp.mean(x * x, -1, keepdims=True) + EPS) * g


def _full_spec(shape):
    return pl.BlockSpec(shape, lambda *_: (0,) * len(shape))


def _mixer_kernel(x_ref, w_in_ref, cw_ref, cvec_ref, bband_ref, lam_ref, cband_ref, svec_ref,
                  wglu_ref, wout_ref, lnvec_ref, o_ref,
                  ha_ref, hb_ref, s_ref, carry_ref, mrg_ref, yc_ref, *, nb, alpha):
    rows = x_ref.shape[0]
    halo = HALO_STEPS * nb
    half = N_STATE // 2

    @pl.when(pl.program_id(0) == 0)
    def _():
        ha_ref[...] = jnp.zeros_like(ha_ref)
        hb_ref[...] = jnp.zeros_like(hb_ref)
        carry_ref[...] = jnp.zeros_like(carry_ref)

    x = x_ref[...]
    proj = jnp.dot(x.astype(BF16), w_in_ref[...], preferred_element_type=F32)

    h = proj[:, :D_CONV] * _sigmoid(proj[:, D_CONV:2 * D_CONV])
    ha_ref[halo:halo + rows, :] = h
    hb_ref[halo + nb:halo + nb + rows, :] = h

    tiles = CONV_CHUNK // SUBLANES
    for lb in range(D_CONV // LANES):
        cols = slice(lb * LANES, (lb + 1) * LANES)
        taps = [jnp.broadcast_to(cw_ref[k:k + 1, cols], (SUBLANES, LANES)) for k in range(CONV_WIDTH)]
        bias = jnp.broadcast_to(cvec_ref[0:1, cols], (SUBLANES, LANES))

        def conv_chunk(c, carry, cols=cols, taps=taps, bias=bias):
            r0 = pl.multiple_of(c * CONV_CHUNK, CONV_CHUNK)
            accs = [bias] * tiles
            for k in range(CONV_WIDTH):
                off = halo + (k - (CONV_WIDTH - 1)) * nb
                src, off = (ha_ref, off) if off % SUBLANES == 0 else (hb_ref, off + nb)
                for i in range(tiles):
                    accs[i] = accs[i] + taps[k] * src[pl.ds(r0 + off + i * SUBLANES, SUBLANES), cols]
            for i in range(tiles):
                yc_ref[pl.ds(r0 + i * SUBLANES, SUBLANES), cols] = accs[i]
            return carry

        lax.fori_loop(0, rows // CONV_CHUNK, conv_chunk, 0)
    ha_ref[0:halo, :] = ha_ref[rows:rows + halo, :]
    hb_ref[0:halo + SUBLANES, :] = hb_ref[rows:rows + halo + SUBLANES, :]
    yc = _silu(_layer_norm(yc_ref[...], cvec_ref[1:2, :], cvec_ref[2:3, :]))
    mrg_ref[:, 0:D_CONV] = _rms_norm(yc, cvec_ref[3:4, :]).astype(BF16)

    u = proj[:, 2 * D_CONV:]
    ub = u.astype(BF16)
    for j in range(2):
        s_ref[j] = jnp.dot(ub[:, j * (D_SSM // 2):(j + 1) * (D_SSM // 2)], bband_ref[j],
                           preferred_element_type=F32)

    first_step = lax.broadcasted_iota(jnp.int32, (SUBLANES, SCAN_LANES), 0) < nb
    for j in range(2):
        for c in range(half // SCAN_LANES):
            re_cols = slice(c * SCAN_LANES, (c + 1) * SCAN_LANES)
            im_cols = slice(half + c * SCAN_LANES, half + (c + 1) * SCAN_LANES)
            lr = jnp.broadcast_to(lam_ref[j:j + 1, re_cols], (SUBLANES, SCAN_LANES))
            li = jnp.broadcast_to(lam_ref[2 + j:3 + j, re_cols], (SUBLANES, SCAN_LANES))

            def scan_tile(i, prev, j=j, re_cols=re_cols, im_cols=im_cols, lr=lr, li=li):
                pre, pim = prev
                r0 = pl.multiple_of(i * SUBLANES, SUBLANES)
                bre = s_ref[j, pl.ds(r0, SUBLANES), re_cols]
                bim = s_ref[j, pl.ds(r0, SUBLANES), im_cols]
                are = lr * pre - li * pim + bre
                aim = lr * pim + li * pre + bim
                rre = pltpu.roll(are, nb, 0)
                rim = pltpu.roll(aim, nb, 0)
                sre = lr * rre - li * rim + bre
                sim = lr * rim + li * rre + bim
                s_ref[j, pl.ds(r0, SUBLANES), re_cols] = jnp.where(first_step, are, sre)
                s_ref[j, pl.ds(r0, SUBLANES), im_cols] = jnp.where(first_step, aim, sim)
                return pltpu.roll(sre, nb, 0), pltpu.roll(sim, nb, 0)

            prev0 = (carry_ref[j, :, re_cols], carry_ref[j, :, im_cols])
            pre, pim = lax.fori_loop(0, rows // SUBLANES, scan_tile, prev0)
            carry_ref[j, :, re_cols] = pre
            carry_ref[j, :, im_cols] = pim

    ys = [jnp.dot(s_ref[j].astype(BF16), cband_ref[j], preferred_element_type=F32) for j in range(2)]
    y = jnp.concatenate(ys, axis=-1) + svec_ref[0:1, :] * u
    z = _gelu_tanh(y)
    z = z * _sigmoid(jnp.dot(z.astype(BF16), wglu_ref[...], preferred_element_type=F32) + svec_ref[1:2, :])
    mrg_ref[:, D_CONV:] = _rms_norm(z, svec_ref[2:3, :]).astype(BF16)

    mix = jnp.dot(mrg_ref[...], wout_ref[...], preferred_element_type=F32)
    o_ref[...] = _layer_norm(alpha * x + mix, lnvec_ref[0:1, :], lnvec_ref[1:2, :])


def _ssm_bands(log_dt, a_re, a_im, b_re, b_im, c_re, c_im):
    g, p, hh = N_SSM_GROUPS, SSM_STATE, SSM_GROUP
    dt = jnp.exp(log_dt.astype(F32))[:, None]
    mag = jnp.exp(a_re * dt)
    ang = a_im * dt
    lb_re = mag * jnp.cos(ang)
    lb_im = mag * jnp.sin(ang)
    den = a_re * a_re + a_im * a_im
    coef_re = ((lb_re - 1.0) * a_re + lb_im * a_im) / den
    coef_im = (lb_im * a_re - (lb_re - 1.0) * a_im) / den
    bb_re = coef_re[..., None] * b_re - coef_im[..., None] * b_im
    bb_im = coef_re[..., None] * b_im + coef_im[..., None] * b_re
    eye = jnp.eye(g // 2, dtype=F32)

    def in_band(bb, j):
        blk = bb[j * (g // 2):(j + 1) * (g // 2)]
        return jnp.einsum('gph,gk->ghkp', blk, eye).reshape(g // 2 * hh, g // 2 * p)

    def out_band(cc, j):
        blk = cc[j * (g // 2):(j + 1) * (g // 2)]
        return jnp.einsum('ghp,gk->gpkh', blk, eye).reshape(g // 2 * p, g // 2 * hh)

    bband = jnp.stack([jnp.concatenate([in_band(bb_re, j), in_band(bb_im, j)], axis=1) for j in range(2)])
    cband = jnp.stack([jnp.concatenate([out_band(c_re, j), -out_band(c_im, j)], axis=0) for j in range(2)])
    lam = jnp.concatenate([lb_re.reshape(2, -1), lb_im.reshape(2, -1)], axis=0)
    return bband.astype(BF16), cband.astype(BF16), lam


def _mixer(x_rows, nb, alpha, w_in, conv_w, conv_b, conv_ln_g, conv_ln_b, log_dt, a_re, a_im, b_re, b_im,
           c_re, c_im, d_skip, w_glu, b_glu, g_conv_out, g_ssm_out, w_out, ln_g, ln_b):
    n_rows, d = x_rows.shape
    steps = n_rows // nb
    tile_steps = min(MIX_STEPS, steps)
    rows = tile_steps * nb
    assert nb * 2 == SUBLANES and steps % tile_steps == 0 and tile_steps >= HALO_STEPS
    assert rows % CONV_CHUNK == 0
    halo = HALO_STEPS * nb
    bband, cband, lam = _ssm_bands(log_dt, a_re, a_im, b_re, b_im, c_re, c_im)
    cvec = jnp.stack([conv_b, conv_ln_g, conv_ln_b, g_conv_out])
    svec = jnp.stack([d_skip, b_glu, g_ssm_out])
    lnvec = jnp.stack([ln_g, ln_b])
    operands = (x_rows, w_in.astype(BF16), conv_w, cvec, bband, lam, cband, svec,
                w_glu.astype(BF16), w_out.astype(BF16), lnvec)
    in_specs = [pl.BlockSpec((rows, d), lambda i: (i, 0))] + [_full_spec(a.shape) for a in operands[1:]]
    return pl.pallas_call(
        functools.partial(_mixer_kernel, nb=nb, alpha=alpha),
        out_shape=jax.ShapeDtypeStruct((n_rows, d), F32),
        grid=(n_rows // rows,),
        in_specs=in_specs,
        out_specs=pl.BlockSpec((rows, d), lambda i: (i, 0)),
        scratch_shapes=[
            pltpu.VMEM((halo + rows, D_CONV), F32),
            pltpu.VMEM((halo + SUBLANES + rows, D_CONV), F32),
            pltpu.VMEM((2, rows, N_STATE), F32),
            pltpu.VMEM((2, SUBLANES, N_STATE), F32),
            pltpu.VMEM((rows, D_CONV + D_SSM), BF16),
            pltpu.VMEM((rows, D_CONV), F32),
        ],
        compiler_params=pltpu.CompilerParams(dimension_semantics=("arbitrary",),
                                             vmem_limit_bytes=VMEM_LIMIT),
        name="mixer",
    )(*operands)


def _kv_kernel(mem_ref, wk_ref, wv_ref, k_ref, v_ref):
    m = mem_ref[0].astype(BF16)
    k_ref[0] = jnp.dot(m, wk_ref[...], preferred_element_type=F32).astype(BF16)
    v_ref[0] = jnp.dot(m, wv_ref[...], preferred_element_type=F32).astype(BF16)


def _kv_proj(mem, wk, wv):
    b, m, d = mem.shape
    blk = pl.BlockSpec((1, m, d), lambda i: (i, 0, 0))
    return pl.pallas_call(
        _kv_kernel,
        out_shape=(jax.ShapeDtypeStruct((b, m, d), BF16), jax.ShapeDtypeStruct((b, m, d), BF16)),
        grid=(b,),
        in_specs=[blk, _full_spec(wk.shape), _full_spec(wv.shape)],
        out_specs=(blk, blk),
        compiler_params=pltpu.CompilerParams(dimension_semantics=("arbitrary",),
                                             vmem_limit_bytes=VMEM_LIMIT),
        name="kv_proj",
    )(mem, wk.astype(BF16), wv.astype(BF16))


def _route(x2, wr_ref, rb_ref):
    t = x2.shape[0]
    logits = lax.dot_general(wr_ref[...], x2.astype(BF16), (((1,), (1,)), ((), ())),
                             preferred_element_type=F32)
    scores = _sigmoid(logits)
    biased = scores + rb_ref[...]
    neg = -jnp.inf

    in_group = lax.broadcasted_iota(jnp.int32, (GROUP_SIZE, t), 0).astype(F32)
    slabs, group_rows = [], []
    for g in range(N_EXPERT_GROUPS):
        slab = biased[g * GROUP_SIZE:(g + 1) * GROUP_SIZE, :]
        m1 = jnp.max(slab, axis=0, keepdims=True)
        i1 = jnp.min(jnp.where(slab == m1, in_group, float(GROUP_SIZE)), axis=0, keepdims=True)
        m2 = jnp.max(jnp.where(in_group == i1, neg, slab), axis=0, keepdims=True)
        slabs.append(slab)
        group_rows.append(m1 + m2)
    group_score = jnp.concatenate(group_rows, axis=0)

    gid = lax.broadcasted_iota(jnp.int32, (N_EXPERT_GROUPS, t), 0).astype(F32)
    chosen = jnp.zeros((N_EXPERT_GROUPS, t), F32)
    for _ in range(TOPK_GROUPS):
        m = jnp.max(group_score, axis=0, keepdims=True)
        gi = jnp.min(jnp.where(group_score == m, gid, float(N_EXPERT_GROUPS)), axis=0, keepdims=True)
        hit = gid == gi
        chosen = jnp.where(hit, 1.0, chosen)
        group_score = jnp.where(hit, neg, group_score)

    masked = jnp.concatenate(
        [jnp.where(chosen[g:g + 1, :] > 0.5, slabs[g], neg) for g in range(N_EXPERT_GROUPS)], axis=0)

    eid = lax.broadcasted_iota(jnp.int32, (N_EXPERTS, t), 0).astype(F32)
    ids, gates = [], []
    for _ in range(TOP_K):
        m = jnp.max(masked, axis=0, keepdims=True)
        ei = jnp.min(jnp.where(masked == m, eid, float(N_EXPERTS)), axis=0, keepdims=True)
        hit = eid == ei
        gates.append(jnp.sum(jnp.where(hit, scores, 0.0), axis=0, keepdims=True))
        ids.append(ei)
        masked = jnp.where(hit, neg, masked)
    ids = jnp.concatenate(ids, axis=0).astype(jnp.int32)
    gates = jnp.concatenate(gates, axis=0)
    gates = ROUTED_SCALE * gates / jnp.sum(gates, axis=0, keepdims=True)
    return ids, gates


def _attn_kernel(x_ref, k_ref, v_ref, wq_ref, wo_ref, lnvec_ref, wr_ref, rb_ref,
                 x2_ref, xt_ref, ids_ref, gates_ref, *, alpha):
    x = x_ref[0]
    q = jnp.dot(x.astype(BF16), wq_ref[...], preferred_element_type=F32)
    heads = []
    for h in range(N_XHEADS):
        cols = slice(h * XHEAD_DIM, (h + 1) * XHEAD_DIM)
        s = lax.dot_general(q[:, cols].astype(BF16), k_ref[0, :, cols], (((1,), (1,)), ((), ())),
                            preferred_element_type=F32) * (XHEAD_DIM ** -0.5)
        p = jnp.exp(s - jnp.max(s, axis=-1, keepdims=True))
        p = p / jnp.sum(p, axis=-1, keepdims=True)
        heads.append(jnp.dot(p.astype(BF16), v_ref[0, :, cols], preferred_element_type=F32))
    o = jnp.concatenate(heads, axis=-1)
    xa = jnp.dot(o.astype(BF16), wo_ref[...], preferred_element_type=F32)
    x2 = _layer_norm(alpha * x + xa, lnvec_ref[0:1, :], lnvec_ref[1:2, :])
    x2_ref[0] = x2
    for s in range(TOKEN_TILE_ROWS):
        xt_ref[0, pl.ds(s, x2.shape[0], stride=TOKEN_TILE_ROWS), :] = x2[:, s * LANES:(s + 1) * LANES]
    ids, gates = _route(x2, wr_ref, rb_ref)
    ids_ref[0] = ids
    gates_ref[0] = gates


def _attention_router(x, k, v, wq, wo, ln_g, ln_b, w_router, router_bias, alpha):
    b, l, d = x.shape
    m = k.shape[1]
    tq = min(ATTN_ROWS, l)
    assert l % tq == 0
    lnvec = jnp.stack([ln_g, ln_b])
    wr_t = w_router.T.astype(BF16)
    rb = router_bias.astype(F32)[:, None]
    assert d == TOKEN_TILE_ROWS * LANES
    tile = pl.BlockSpec((1, tq, d), lambda i, j: (i, j, 0))
    tok_tile = pl.BlockSpec((1, tq * TOKEN_TILE_ROWS, LANES), lambda i, j: (i, j, 0))
    mem_blk = pl.BlockSpec((1, m, d), lambda i, j: (i, 0, 0))
    topk_blk = pl.BlockSpec((1, TOP_K, tq), lambda i, j: (i, 0, j))
    return pl.pallas_call(
        functools.partial(_attn_kernel, alpha=alpha),
        out_shape=(jax.ShapeDtypeStruct((b, l, d), F32),
                   jax.ShapeDtypeStruct((b, l * TOKEN_TILE_ROWS, LANES), F32),
                   jax.ShapeDtypeStruct((b, TOP_K, l), jnp.int32),
                   jax.ShapeDtypeStruct((b, TOP_K, l), F32)),
        grid=(b, l // tq),
        in_specs=[tile, mem_blk, mem_blk, _full_spec(wq.shape), _full_spec(wo.shape),
                  _full_spec(lnvec.shape), _full_spec(wr_t.shape), _full_spec(rb.shape)],
        out_specs=(tile, tok_tile, topk_blk, topk_blk),
        compiler_params=pltpu.CompilerParams(dimension_semantics=("arbitrary", "arbitrary"),
                                             vmem_limit_bytes=VMEM_LIMIT),
        name="attention_router",
    )(x, k, v, wq.astype(BF16), wo.astype(BF16), lnvec, wr_t, rb)


def _moe_kernel(starts_ref, counts_ref, dest_hbm, gates_hbm, x_hbm, wg_hbm, wu_hbm, wd_hbm, out_hbm,
                xs_ref, acc_ref, lhs_ref, y_ref, wg_buf, wu_buf, wd_buf, wgu_ref, wdn_ref,
                dest_ref, gsrc_ref, list_ref, sem, wsem):
    part = pl.program_id(0)
    n_e = wg_hbm.shape[0]
    tr = TOKEN_TILE_ROWS
    n_rows = out_hbm.shape[1]
    n_tok = n_rows // tr
    n_list = list_ref.shape[1] - LANES
    spare_code = n_tok * TOP_K

    def code_rows(code):
        token = lax.shift_right_logical(code, TOP_K.bit_length() - 1)
        return pl.ds(pl.multiple_of(token * tr, tr), tr)

    def tile_rows(i):
        return pl.ds(pl.multiple_of(i * tr, tr), tr)

    n_slots = wg_buf.shape[0]

    def weight_copies(e, slot):
        copies = []
        for i, (src, dst) in enumerate(((wg_hbm, wg_buf), (wu_hbm, wu_buf), (wd_hbm, wd_buf))):
            rows_c = src.shape[1] // WEIGHT_DMA_CHUNKS
            for c in range(WEIGHT_DMA_CHUNKS):
                rows = pl.ds(c * rows_c, rows_c)
                copies.append(pltpu.make_async_copy(src.at[e, rows], dst.at[slot, rows], wsem.at[slot, i]))
        return copies

    for ahead in range(n_slots - 1):
        for cp in weight_copies(ahead, ahead):
            cp.start()
    load = pltpu.make_async_copy(x_hbm.at[part], xs_ref.at[pl.ds(0, n_rows)], sem.at[0])
    load_dest = pltpu.make_async_copy(dest_hbm.at[part], dest_ref, sem.at[2])
    load_gates = pltpu.make_async_copy(gates_hbm.at[part], gsrc_ref.at[:, pl.ds(0, spare_code)], sem.at[3])
    load.start()
    load_dest.start()
    load_gates.start()
    acc_ref[...] = jnp.zeros_like(acc_ref)
    lhs_ref[...] = jnp.zeros_like(lhs_ref)
    xs_ref[pl.ds(n_rows, tr), :] = jnp.zeros((tr, LANES), F32)
    gsrc_ref[0, spare_code] = 0.0

    def fill_rounding(ex, c):
        first = starts_ref[part * n_e + ex] + counts_ref[part * n_e + ex]
        n_fill = (SUBLANES - first % SUBLANES) % SUBLANES
        for j in range(SUBLANES - 1):
            list_ref[0, jnp.where(j < n_fill, first + j, n_list + j)] = spare_code
        return c

    lax.fori_loop(0, n_e, fill_rounding, 0)
    load_dest.wait()

    def place(t, c):
        for k in range(TOP_K):
            list_ref[0, dest_ref[k, t]] = t * TOP_K + k
        return c

    lax.fori_loop(0, n_tok, place, 0)
    load_gates.wait()
    load.wait()

    def run_block(i0, n_groups, size):
        def gather(g, c):
            base = pl.multiple_of(g * SUBLANES, SUBLANES)
            for r in range(SUBLANES):
                lhs_ref[tile_rows(base + r), :] = xs_ref[code_rows(list_ref[0, i0 + base + r]), :]
            return c

        lax.fori_loop(0, n_groups, gather, 0)
        xb = jnp.concatenate([lhs_ref[pl.ds(s, size, stride=tr), :] for s in range(tr)], axis=-1)
        gu = jnp.dot(xb.astype(BF16), wgu_ref[...], preferred_element_type=F32)
        hid = _silu(gu[:, :D_EXPERT]) * gu[:, D_EXPERT:]
        y = jnp.dot(hid.astype(BF16), wdn_ref[...], preferred_element_type=F32)
        for s in range(tr):
            y_ref[pl.ds(s, size, stride=tr), :] = y[:, s * LANES:(s + 1) * LANES]

        def scatter(g, c):
            base = pl.multiple_of(g * SUBLANES, SUBLANES)
            rows, vals = [], []
            for r in range(SUBLANES):
                code = list_ref[0, i0 + base + r]
                rows.append(code_rows(code))
                vals.append(acc_ref[rows[-1], :] + gsrc_ref[0, code] * y_ref[tile_rows(base + r), :])
            for row, val in zip(rows, vals):
                acc_ref[row, :] = val
            return c

        lax.fori_loop(0, n_groups, scatter, 0)

    def expert(e, c):
        slot = e % n_slots
        ahead = e + n_slots - 1

        @pl.when(ahead < n_e)
        def _():
            for cp in weight_copies(ahead, ahead % n_slots):
                cp.start()

        for cp in weight_copies(e, slot):
            cp.wait()
        start = starts_ref[part * n_e + e]
        count = counts_ref[part * n_e + e]

        @pl.when(count > 0)
        def _():
            wgu_ref[:, :D_EXPERT] = wg_buf[slot].astype(BF16)
            wgu_ref[:, D_EXPERT:] = wu_buf[slot].astype(BF16)
            wdn_ref[...] = wd_buf[slot].astype(BF16)
            n_full = count // MOE_ROWS

            def full_block(bi, cc):
                run_block(start + bi * MOE_ROWS, MOE_ROWS // SUBLANES, MOE_ROWS)
                return cc

            lax.fori_loop(0, n_full, full_block, 0)
            rest = count - n_full * MOE_ROWS
            rest_groups = (rest + SUBLANES - 1) // SUBLANES

            @pl.when(rest > MOE_ROWS // 2)
            def _():
                run_block(start + n_full * MOE_ROWS, rest_groups, MOE_ROWS)

            @pl.when((rest > 0) & (rest <= MOE_ROWS // 2))
            def _():
                run_block(start + n_full * MOE_ROWS, rest_groups, MOE_ROWS // 2)

        return c

    lax.fori_loop(0, n_e, expert, 0)
    store = pltpu.make_async_copy(acc_ref.at[pl.ds(0, n_rows)], out_hbm.at[part], sem.at[1])
    store.start()
    store.wait()


def _routed_experts(xt, dest, gates, starts, counts, w_gate, w_up, w_down):
    p, n_rows, lanes = xt.shape
    n_e, d, d_e = w_gate.shape
    k, l = dest.shape[1:]
    n_list = l * k + n_e * SUBLANES
    grid_spec = pltpu.PrefetchScalarGridSpec(
        num_scalar_prefetch=2,
        grid=(p,),
        in_specs=[pl.BlockSpec(memory_space=pl.ANY)] * 6,
        out_specs=pl.BlockSpec(memory_space=pl.ANY),
        scratch_shapes=[
            pltpu.VMEM((n_rows + TOKEN_TILE_ROWS, lanes), F32),
            pltpu.VMEM((n_rows + TOKEN_TILE_ROWS, lanes), F32),
            pltpu.VMEM((MOE_ROWS * TOKEN_TILE_ROWS, lanes), F32),
            pltpu.VMEM((MOE_ROWS * TOKEN_TILE_ROWS, lanes), F32),
            pltpu.VMEM((WEIGHT_SLOTS, d, d_e), F32),
            pltpu.VMEM((WEIGHT_SLOTS, d, d_e), F32),
            pltpu.VMEM((WEIGHT_SLOTS, d_e, d), F32),
            pltpu.VMEM((d, 2 * d_e), BF16),
            pltpu.VMEM((d_e, d), BF16),
            pltpu.SMEM((k, l), jnp.int32),
            pltpu.SMEM((1, l * k + LANES), F32),
            pltpu.SMEM((1, n_list + LANES), jnp.int32),
            pltpu.SemaphoreType.DMA((4,)),
            pltpu.SemaphoreType.DMA((WEIGHT_SLOTS, 3)),
        ])
    return pl.pallas_call(
        _moe_kernel,
        out_shape=jax.ShapeDtypeStruct((p, n_rows, lanes), F32),
        grid_spec=grid_spec,
        compiler_params=pltpu.CompilerParams(dimension_semantics=("arbitrary",),
                                             vmem_limit_bytes=VMEM_LIMIT),
        name="routed_experts",
    )(starts, counts, dest, gates, xt, w_gate, w_up, w_down)


def _dispatch_kernel(ids_ref, dest_ref, starts_ref, counts_ref, cnt_ref, *, chunk):
    n_tok = ids_ref.shape[2]
    n_chunks = n_tok // chunk
    eid = lax.broadcasted_iota(jnp.int32, (N_EXPERTS, chunk), 0)

    def one_hots(c):
        cols = pl.ds(pl.multiple_of(c * chunk, chunk), chunk)
        return [jnp.where(eid == ids_ref[0, k:k + 1, cols], 1.0, 0.0) for k in range(TOP_K)], cols

    def count_chunk(c, carry):
        hots, _ = one_hots(c)
        cnt_ref[...] += jnp.sum(sum(hots), axis=1, keepdims=True)
        return carry

    cnt_ref[...] = jnp.zeros_like(cnt_ref)
    lax.fori_loop(0, n_chunks, count_chunk, 0)
    counts = cnt_ref[...]
    counts_ref[0] = counts.astype(jnp.int32)

    rounded = jnp.floor((counts + (SUBLANES - 1)) * (1.0 / SUBLANES)) * SUBLANES
    row = lax.broadcasted_iota(jnp.int32, (N_EXPERTS, LANES), 0)
    scan = jnp.broadcast_to(rounded, (N_EXPERTS, LANES))
    shift = 1
    while shift < N_EXPERTS:
        scan = scan + jnp.where(row >= shift, pltpu.roll(scan, shift, 0), 0.0)
        shift *= 2
    starts = scan[:, 0:1] - rounded
    starts_ref[0] = starts.astype(jnp.int32)

    upper = jnp.where(lax.broadcasted_iota(jnp.int32, (chunk, chunk), 0)
                      < lax.broadcasted_iota(jnp.int32, (chunk, chunk), 1), 1.0, 0.0).astype(BF16)

    def place_chunk(c, seen):
        hots, cols = one_hots(c)
        mask = sum(hots)
        before = jnp.dot(mask.astype(BF16), upper, preferred_element_type=F32)
        pos = starts + seen + before
        for k in range(TOP_K):
            dest_ref[0, k:k + 1, cols] = jnp.sum(hots[k] * pos, axis=0, keepdims=True).astype(jnp.int32)
        return seen + jnp.sum(mask, axis=1, keepdims=True)

    lax.fori_loop(0, n_chunks, place_chunk, jnp.zeros((N_EXPERTS, 1), F32))


def _dispatch(ids):
    p, k, l = ids.shape
    chunk = min(256, l)
    assert l % chunk == 0
    ids_blk = pl.BlockSpec((1, k, l), lambda i: (i, 0, 0))
    col_blk = pl.BlockSpec((1, N_EXPERTS, 1), lambda i: (i, 0, 0))
    dest, starts, counts = pl.pallas_call(
        functools.partial(_dispatch_kernel, chunk=chunk),
        out_shape=(jax.ShapeDtypeStruct((p, k, l), jnp.int32),
                   jax.ShapeDtypeStruct((p, N_EXPERTS, 1), jnp.int32),
                   jax.ShapeDtypeStruct((p, N_EXPERTS, 1), jnp.int32)),
        grid=(p,),
        in_specs=[ids_blk],
        out_specs=(ids_blk, col_blk, col_blk),
        scratch_shapes=[pltpu.VMEM((N_EXPERTS, 1), F32)],
        compiler_params=pltpu.CompilerParams(dimension_semantics=("arbitrary",),
                                             vmem_limit_bytes=VMEM_LIMIT),
        name="dispatch",
    )(ids)
    return dest, starts.reshape(-1), counts.reshape(-1)


def _ffn_out_kernel(x_ref, routed_ref, wgu_ref, wdn_ref, lnvec_ref, o_ref, *, alpha):
    x = x_ref[...]
    rows = x.shape[0]
    routed = jnp.concatenate([routed_ref[pl.ds(s, rows, stride=TOKEN_TILE_ROWS), :]
                              for s in range(TOKEN_TILE_ROWS)], axis=-1)
    gu = jnp.dot(x.astype(BF16), wgu_ref[...], preferred_element_type=F32)
    d_sh = wdn_ref.shape[0]
    hid = _silu(gu[:, :d_sh]) * gu[:, d_sh:]
    shared = jnp.dot(hid.astype(BF16), wdn_ref[...], preferred_element_type=F32)
    o_ref[...] = _layer_norm(alpha * x + routed + shared, lnvec_ref[0:1, :], lnvec_ref[1:2, :])


def _ffn_out(x, routed_tiles, sh_gate, sh_up, sh_down, ln_g, ln_b, alpha):
    n, d = x.shape
    rows = min(FFN_ROWS, n)
    assert n % rows == 0
    wgu = jnp.concatenate([sh_gate, sh_up], axis=1).astype(BF16)
    lnvec = jnp.stack([ln_g, ln_b])
    tile = pl.BlockSpec((rows, d), lambda i: (i, 0))
    tok_tile = pl.BlockSpec((rows * TOKEN_TILE_ROWS, LANES), lambda i: (i, 0))
    return pl.pallas_call(
        functools.partial(_ffn_out_kernel, alpha=alpha),
        out_shape=jax.ShapeDtypeStruct((n, d), F32),
        grid=(n // rows,),
        in_specs=[tile, tok_tile, _full_spec(wgu.shape), _full_spec(sh_down.shape), _full_spec(lnvec.shape)],
        out_specs=tile,
        compiler_params=pltpu.CompilerParams(dimension_semantics=("arbitrary",),
                                             vmem_limit_bytes=VMEM_LIMIT),
        name="shared_ffn_norm",
    )(x, routed_tiles, wgu, sh_down.astype(BF16), lnvec)


def kernel(x, mem, w_in, conv_w, conv_b, conv_ln_g, conv_ln_b, ssm_log_dt, ssm_a_re, ssm_a_im, ssm_b_re, ssm_b_im, ssm_c_re, ssm_c_im, ssm_d, w_glu, b_glu, g_conv_out, g_ssm_out, w_out, ln1_g, ln1_b, wq, wk, wv, wo, ln2_g, ln2_b, w_router, router_bias, w_gate, w_up, w_down, sh_gate, sh_up, sh_down, ln3_g, ln3_b):
    depth = w_in.shape[0]
    alpha = (2 * depth) ** 0.25
    b, l, d = x.shape
    for i in range(depth):
        x_rows = jnp.swapaxes(x, 0, 1).reshape(l * b, d)
        x_rows = _mixer(x_rows, b, alpha, w_in[i], conv_w[i], conv_b[i], conv_ln_g[i], conv_ln_b[i],
                        ssm_log_dt[i], ssm_a_re[i], ssm_a_im[i], ssm_b_re[i], ssm_b_im[i],
                        ssm_c_re[i], ssm_c_im[i], ssm_d[i], w_glu[i], b_glu[i],
                        g_conv_out[i], g_ssm_out[i], w_out[i], ln1_g[i], ln1_b[i])
        x = jnp.swapaxes(x_rows.reshape(l, b, d), 0, 1)
        k, v = _kv_proj(mem, wk[i], wv[i])
        x, xt, ids, gates = _attention_router(x, k, v, wq[i], wo[i], ln2_g[i], ln2_b[i],
                                              w_router[i], router_bias[i], alpha)
        dest, starts, counts = _dispatch(ids)
        gates_tm = jnp.swapaxes(gates, 1, 2).reshape(b, 1, l * TOP_K)
        routed = _routed_experts(xt, dest, gates_tm, starts, counts, w_gate[i], w_up[i], w_down[i])
        x = _ffn_out(x.reshape(b * l, d), routed.reshape(b * l * TOKEN_TILE_ROWS, LANES),
                     sh_gate[i], sh_up[i], sh_down[i], ln3_g[i], ln3_b[i], alpha).reshape(b, l, d)
    return x
```

```python
import functools

import jax
import jax.numpy as jnp
from jax import lax
from jax.experimental import pallas as pl
from jax.experimental.pallas import tpu as pltpu

F32 = jnp.float32
BF16 = jnp.bfloat16

D_MODEL = 1024
D_CONV = 512
D_SSM = 512
CONV_WIDTH = 31
SSM_GROUP = 16
N_SSM_GROUPS = D_SSM // SSM_GROUP
SSM_STATE = 64
N_STATE = N_SSM_GROUPS * SSM_STATE
N_XHEADS = 4
XHEAD_DIM = D_MODEL // N_XHEADS
N_EXPERTS = 256
TOP_K = 8
N_EXPERT_GROUPS = 8
GROUP_SIZE = N_EXPERTS // N_EXPERT_GROUPS
TOPK_GROUPS = 4
D_EXPERT = 256
ROUTED_SCALE = 2.5
EPS = 1e-5

SUBLANES = 8
LANES = 128
TOKEN_TILE_ROWS = D_MODEL // LANES
HALO_STEPS = 32
MIX_STEPS = 128
CONV_CHUNK = 64
SCAN_LANES = 512
ATTN_ROWS = 512
MOE_ROWS = 256
FFN_ROWS = 512
WEIGHT_SLOTS = 3
WEIGHT_DMA_CHUNKS = 4
VMEM_LIMIT = 56 * 1024 * 1024


def _sigmoid(x):
    return 1.0 / (1.0 + jnp.exp(-x))


def _silu(x):
    return x * _sigmoid(x)


def _gelu_tanh(x):
    c = 0.7978845608028654
    return 0.5 * x * (1.0 + jnp.tanh(c * (x + 0.044715 * (x * x * x))))


def _layer_norm(x, g, b):
    mu = jnp.mean(x, -1, keepdims=True)
    xc = x - mu
    var = jnp.mean(xc * xc, -1, keepdims=True)
    return xc * lax.rsqrt(var + EPS) * g + b


def _rms_norm(x, g):
    return x * lax.rsqrt(jnp.mean(x * x, -1, keepdims=True) + EPS) * g


def _full_spec(shape):
    return pl.BlockSpec(shape, lambda *_: (0,) * len(shape))


def _mixer_kernel(x_ref, w_in_ref, cw_ref, cvec_ref, bband_ref, lam_ref, cband_ref, svec_ref,
                  wglu_ref, wout_ref, lnvec_ref, o_ref,
                  ha_ref, hb_ref, s_ref, carry_ref, mrg_ref, yc_ref, *, nb, alpha):
    rows = x_ref.shape[0]
    halo = HALO_STEPS * nb
    half = N_STATE // 2

    @pl.when(pl.program_id(0) == 0)
    def _():
        ha_ref[...] = jnp.zeros_like(ha_ref)
        hb_ref[...] = jnp.zeros_like(hb_ref)
        carry_ref[...] = jnp.zeros_like(carry_ref)

    x = x_ref[...]
    proj = jnp.dot(x.astype(BF16), w_in_ref[...], preferred_element_type=F32)

    h = proj[:, :D_CONV] * _sigmoid(proj[:, D_CONV:2 * D_CONV])
    ha_ref[halo:halo + rows, :] = h
    hb_ref[halo + nb:halo + nb + rows, :] = h

    tiles = CONV_CHUNK // SUBLANES
    for lb in range(D_CONV // LANES):
        cols = slice(lb * LANES, (lb + 1) * LANES)
        taps = [jnp.broadcast_to(cw_ref[k:k + 1, cols], (SUBLANES, LANES)) for k in range(CONV_WIDTH)]
        bias = jnp.broadcast_to(cvec_ref[0:1, cols], (SUBLANES, LANES))

        def conv_chunk(c, carry, cols=cols, taps=taps, bias=bias):
            r0 = pl.multiple_of(c * CONV_CHUNK, CONV_CHUNK)
            accs = [bias] * tiles
            for k in range(CONV_WIDTH):
                off = halo + (k - (CONV_WIDTH - 1)) * nb
                src, off = (ha_ref, off) if off % SUBLANES == 0 else (hb_ref, off + nb)
                for i in range(tiles):
                    accs[i] = accs[i] + taps[k] * src[pl.ds(r0 + off + i * SUBLANES, SUBLANES), cols]
            for i in range(tiles):
                yc_ref[pl.ds(r0 + i * SUBLANES, SUBLANES), cols] = accs[i]
            return carry

        lax.fori_loop(0, rows // CONV_CHUNK, conv_chunk, 0)
    ha_ref[0:halo, :] = ha_ref[rows:rows + halo, :]
    hb_ref[0:halo + SUBLANES, :] = hb_ref[rows:rows + halo + SUBLANES, :]
    yc = _silu(_layer_norm(yc_ref[...], cvec_ref[1:2, :], cvec_ref[2:3, :]))
    mrg_ref[:, 0:D_CONV] = _rms_norm(yc, cvec_ref[3:4, :]).astype(BF16)

    u = proj[:, 2 * D_CONV:]
    ub = u.astype(BF16)
    for j in range(2):
        s_ref[j] = jnp.dot(ub[:, j * (D_SSM // 2):(j + 1) * (D_SSM // 2)], bband_ref[j],
                           preferred_element_type=F32)

    first_step = lax.broadcasted_iota(jnp.int32, (SUBLANES, SCAN_LANES), 0) < nb
    for j in range(2):
        for c in range(half // SCAN_LANES):
            re_cols = slice(c * SCAN_LANES, (c + 1) * SCAN_LANES)
            im_cols = slice(half + c * SCAN_LANES, half + (c + 1) * SCAN_LANES)
            lr = jnp.broadcast_to(lam_ref[j:j + 1, re_cols], (SUBLANES, SCAN_LANES))
            li = jnp.broadcast_to(lam_ref[2 + j:3 + j, re_cols], (SUBLANES, SCAN_LANES))

            def scan_tile(i, prev, j=j, re_cols=re_cols, im_cols=im_cols, lr=lr, li=li):
                pre, pim = prev
                r0 = pl.multiple_of(i * SUBLANES, SUBLANES)
                bre = s_ref[j, pl.ds(r0, SUBLANES), re_cols]
                bim = s_ref[j, pl.ds(r0, SUBLANES), im_cols]
                are = lr * pre - li * pim + bre
                aim = lr * pim + li * pre + bim
                rre = pltpu.roll(are, nb, 0)
                rim = pltpu.roll(aim, nb, 0)
                sre = lr * rre - li * rim + bre
                sim = lr * rim + li * rre + bim
                s_ref[j, pl.ds(r0, SUBLANES), re_cols] = jnp.where(first_step, are, sre)
                s_ref[j, pl.ds(r0, SUBLANES), im_cols] = jnp.where(first_step, aim, sim)
                return pltpu.roll(sre, nb, 0), pltpu.roll(sim, nb, 0)

            prev0 = (carry_ref[j, :, re_cols], carry_ref[j, :, im_cols])
            pre, pim = lax.fori_loop(0, rows // SUBLANES, scan_tile, prev0)
            carry_ref[j, :, re_cols] = pre
            carry_ref[j, :, im_cols] = pim

    ys = [jnp.dot(s_ref[j].astype(BF16), cband_ref[j], preferred_element_type=F32) for j in range(2)]
    y = jnp.concatenate(ys, axis=-1) + svec_ref[0:1, :] * u
    z = _gelu_tanh(y)
    z = z * _sigmoid(jnp.dot(z.astype(BF16), wglu_ref[...], preferred_element_type=F32) + svec_ref[1:2, :])
    mrg_ref[:, D_CONV:] = _rms_norm(z, svec_ref[2:3, :]).astype(BF16)

    mix = jnp.dot(mrg_ref[...], wout_ref[...], preferred_element_type=F32)
    o_ref[...] = _layer_norm(alpha * x + mix, lnvec_ref[0:1, :], lnvec_ref[1:2, :])


def _ssm_bands(log_dt, a_re, a_im, b_re, b_im, c_re, c_im):
    g, p, hh = N_SSM_GROUPS, SSM_STATE, SSM_GROUP
    dt = jnp.exp(log_dt.astype(F32))[:, None]
    mag = jnp.exp(a_re * dt)
    ang = a_im * dt
    lb_re = mag * jnp.cos(ang)
    lb_im = mag * jnp.sin(ang)
    den = a_re * a_re + a_im * a_im
    coef_re = ((lb_re - 1.0) * a_re + lb_im * a_im) / den
    coef_im = (lb_im * a_re - (lb_re - 1.0) * a_im) / den
    bb_re = coef_re[..., None] * b_re - coef_im[..., None] * b_im
    bb_im = coef_re[..., None] * b_im + coef_im[..., None] * b_re
    eye = jnp.eye(g // 2, dtype=F32)

    def in_band(bb, j):
        blk = bb[j * (g // 2):(j + 1) * (g // 2)]
        return jnp.einsum('gph,gk->ghkp', blk, eye).reshape(g // 2 * hh, g // 2 * p)

    def out_band(cc, j):
        blk = cc[j * (g // 2):(j + 1) * (g // 2)]
        return jnp.einsum('ghp,gk->gpkh', blk, eye).reshape(g // 2 * p, g // 2 * hh)

    bband = jnp.stack([jnp.concatenate([in_band(bb_re, j), in_band(bb_im, j)], axis=1) for j in range(2)])
    cband = jnp.stack([jnp.concatenate([out_band(c_re, j), -out_band(c_im, j)], axis=0) for j in range(2)])
    lam = jnp.concatenate([lb_re.reshape(2, -1), lb_im.reshape(2, -1)], axis=0)
    return bband.astype(BF16), cband.astype(BF16), lam


def _mixer(x_rows, nb, alpha, w_in, conv_w, conv_b, conv_ln_g, conv_ln_b, log_dt, a_re, a_im, b_re, b_im,
           c_re, c_im, d_skip, w_glu, b_glu, g_conv_out, g_ssm_out, w_out, ln_g, ln_b):
    n_rows, d = x_rows.shape
    steps = n_rows // nb
    tile_steps = min(MIX_STEPS, steps)
    rows = tile_steps * nb
    assert nb * 2 == SUBLANES and steps % tile_steps == 0 and tile_steps >= HALO_STEPS
    assert rows % CONV_CHUNK == 0
    halo = HALO_STEPS * nb
    bband, cband, lam = _ssm_bands(log_dt, a_re, a_im, b_re, b_im, c_re, c_im)
    cvec = jnp.stack([conv_b, conv_ln_g, conv_ln_b, g_conv_out])
    svec = jnp.stack([d_skip, b_glu, g_ssm_out])
    lnvec = jnp.stack([ln_g, ln_b])
    operands = (x_rows, w_in.astype(BF16), conv_w, cvec, bband, lam, cband, svec,
                w_glu.astype(BF16), w_out.astype(BF16), lnvec)
    in_specs = [pl.BlockSpec((rows, d), lambda i: (i, 0))] + [_full_spec(a.shape) for a in operands[1:]]
    return pl.pallas_call(
        functools.partial(_mixer_kernel, nb=nb, alpha=alpha),
        out_shape=jax.ShapeDtypeStruct((n_rows, d), F32),
        grid=(n_rows // rows,),
        in_specs=in_specs,
        out_specs=pl.BlockSpec((rows, d), lambda i: (i, 0)),
        scratch_shapes=[
            pltpu.VMEM((halo + rows, D_CONV), F32),
            pltpu.VMEM((halo + SUBLANES + rows, D_CONV), F32),
            pltpu.VMEM((2, rows, N_STATE), F32),
            pltpu.VMEM((2, SUBLANES, N_STATE), F32),
            pltpu.VMEM((rows, D_CONV + D_SSM), BF16),
            pltpu.VMEM((rows, D_CONV), F32),
        ],
        compiler_params=pltpu.CompilerParams(dimension_semantics=("arbitrary",),
                                             vmem_limit_bytes=VMEM_LIMIT),
        name="mixer",
    )(*operands)


def _kv_kernel(mem_ref, wk_ref, wv_ref, k_ref, v_ref):
    m = mem_ref[0].astype(BF16)
    k_ref[0] = jnp.dot(m, wk_ref[...], preferred_element_type=F32).astype(BF16)
    v_ref[0] = jnp.dot(m, wv_ref[...], preferred_element_type=F32).astype(BF16)


def _kv_proj(mem, wk, wv):
    b, m, d = mem.shape
    blk = pl.BlockSpec((1, m, d), lambda i: (i, 0, 0))
    return pl.pallas_call(
        _kv_kernel,
        out_shape=(jax.ShapeDtypeStruct((b, m, d), BF16), jax.ShapeDtypeStruct((b, m, d), BF16)),
        grid=(b,),
        in_specs=[blk, _full_spec(wk.shape), _full_spec(wv.shape)],
        out_specs=(blk, blk),
        compiler_params=pltpu.CompilerParams(dimension_semantics=("arbitrary",),
                                             vmem_limit_bytes=VMEM_LIMIT),
        name="kv_proj",
    )(mem, wk.astype(BF16), wv.astype(BF16))


def _route(x2, wr_ref, rb_ref):
    t = x2.shape[0]
    logits = lax.dot_general(wr_ref[...], x2.astype(BF16), (((1,), (1,)), ((), ())),
                             preferred_element_type=F32)
    scores = _sigmoid(logits)
    biased = scores + rb_ref[...]
    neg = -jnp.inf

    in_group = lax.broadcasted_iota(jnp.int32, (GROUP_SIZE, t), 0).astype(F32)
    slabs, group_rows = [], []
    for g in range(N_EXPERT_GROUPS):
        slab = biased[g * GROUP_SIZE:(g + 1) * GROUP_SIZE, :]
        m1 = jnp.max(slab, axis=0, keepdims=True)
        i1 = jnp.min(jnp.where(slab == m1, in_group, float(GROUP_SIZE)), axis=0, keepdims=True)
        m2 = jnp.max(jnp.where(in_group == i1, neg, slab), axis=0, keepdims=True)
        slabs.append(slab)
        group_rows.append(m1 + m2)
    group_score = jnp.concatenate(group_rows, axis=0)

    gid = lax.broadcasted_iota(jnp.int32, (N_EXPERT_GROUPS, t), 0).astype(F32)
    chosen = jnp.zeros((N_EXPERT_GROUPS, t), F32)
    for _ in range(TOPK_GROUPS):
        m = jnp.max(group_score, axis=0, keepdims=True)
        gi = jnp.min(jnp.where(group_score == m, gid, float(N_EXPERT_GROUPS)), axis=0, keepdims=True)
        hit = gid == gi
        chosen = jnp.where(hit, 1.0, chosen)
        group_score = jnp.where(hit, neg, group_score)

    masked = jnp.concatenate(
        [jnp.where(chosen[g:g + 1, :] > 0.5, slabs[g], neg) for g in range(N_EXPERT_GROUPS)], axis=0)

    eid = lax.broadcasted_iota(jnp.int32, (N_EXPERTS, t), 0).astype(F32)
    ids, gates = [], []
    for _ in range(TOP_K):
        m = jnp.max(masked, axis=0, keepdims=True)
        ei = jnp.min(jnp.where(masked == m, eid, float(N_EXPERTS)), axis=0, keepdims=True)
        hit = eid == ei
        gates.append(jnp.sum(jnp.where(hit, scores, 0.0), axis=0, keepdims=True))
        ids.append(ei)
        masked = jnp.where(hit, neg, masked)
    ids = jnp.concatenate(ids, axis=0).astype(jnp.int32)
    gates = jnp.concatenate(gates, axis=0)
    gates = ROUTED_SCALE * gates / jnp.sum(gates, axis=0, keepdims=True)
    return ids, gates


def _attn_kernel(x_ref, k_ref, v_ref, wq_ref, wo_ref, lnvec_ref, wr_ref, rb_ref,
                 x2_ref, xt_ref, ids_ref, gates_ref, *, alpha):
    x = x_ref[0]
    q = jnp.dot(x.astype(BF16), wq_ref[...], preferred_element_type=F32)
    heads = []
    for h in range(N_XHEADS):
        cols = slice(h * XHEAD_DIM, (h + 1) * XHEAD_DIM)
        s = lax.dot_general(q[:, cols].astype(BF16), k_ref[0, :, cols], (((1,), (1,)), ((), ())),
                            preferred_element_type=F32) * (XHEAD_DIM ** -0.5)
        p = jnp.exp(s - jnp.max(s, axis=-1, keepdims=True))
        p = p / jnp.sum(p, axis=-1, keepdims=True)
        heads.append(jnp.dot(p.astype(BF16), v_ref[0, :, cols], preferred_element_type=F32))
    o = jnp.concatenate(heads, axis=-1)
    xa = jnp.dot(o.astype(BF16), wo_ref[...], preferred_element_type=F32)
    x2 = _layer_norm(alpha * x + xa, lnvec_ref[0:1, :], lnvec_ref[1:2, :])
    x2_ref[0] = x2
    for s in range(TOKEN_TILE_ROWS):
        xt_ref[0, pl.ds(s, x2.shape[0], stride=TOKEN_TILE_ROWS), :] = x2[:, s * LANES:(s + 1) * LANES]
    ids, gates = _route(x2, wr_ref, rb_ref)
    ids_ref[0] = ids
    gates_ref[0] = gates


def _attention_router(x, k, v, wq, wo, ln_g, ln_b, w_router, router_bias, alpha):
    b, l, d = x.shape
    m = k.shape[1]
    tq = min(ATTN_ROWS, l)
    assert l % tq == 0
    lnvec = jnp.stack([ln_g, ln_b])
    wr_t = w_router.T.astype(BF16)
    rb = router_bias.astype(F32)[:, None]
    assert d == TOKEN_TILE_ROWS * LANES
    tile = pl.BlockSpec((1, tq, d), lambda i, j: (i, j, 0))
    tok_tile = pl.BlockSpec((1, tq * TOKEN_TILE_ROWS, LANES), lambda i, j: (i, j, 0))
    mem_blk = pl.BlockSpec((1, m, d), lambda i, j: (i, 0, 0))
    topk_blk = pl.BlockSpec((1, TOP_K, tq), lambda i, j: (i, 0, j))
    return pl.pallas_call(
        functools.partial(_attn_kernel, alpha=alpha),
        out_shape=(jax.ShapeDtypeStruct((b, l, d), F32),
                   jax.ShapeDtypeStruct((b, l * TOKEN_TILE_ROWS, LANES), F32),
                   jax.ShapeDtypeStruct((b, TOP_K, l), jnp.int32),
                   jax.ShapeDtypeStruct((b, TOP_K, l), F32)),
        grid=(b, l // tq),
        in_specs=[tile, mem_blk, mem_blk, _full_spec(wq.shape), _full_spec(wo.shape),
                  _full_spec(lnvec.shape), _full_spec(wr_t.shape), _full_spec(rb.shape)],
        out_specs=(tile, tok_tile, topk_blk, topk_blk),
        compiler_params=pltpu.CompilerParams(dimension_semantics=("arbitrary", "arbitrary"),
                                             vmem_limit_bytes=VMEM_LIMIT),
        name="attention_router",
    )(x, k, v, wq.astype(BF16), wo.astype(BF16), lnvec, wr_t, rb)


def _moe_kernel(starts_ref, counts_ref, dest_hbm, gates_hbm, x_hbm, wg_hbm, wu_hbm, wd_hbm, out_hbm,
                xs_ref, acc_ref, lhs_ref, y_ref, wg_buf, wu_buf, wd_buf, wgu_ref, wdn_ref,
                dest_ref, gsrc_ref, list_ref, sem, wsem):
    part = pl.program_id(0)
    n_e = wg_hbm.shape[0]
    tr = TOKEN_TILE_ROWS
    n_rows = out_hbm.shape[1]
    n_tok = n_rows // tr
    n_list = list_ref.shape[1] - LANES
    spare_code = n_tok * TOP_K

    def code_rows(code):
        token = lax.shift_right_logical(code, TOP_K.bit_length() - 1)
        return pl.ds(pl.multiple_of(token * tr, tr), tr)

    def tile_rows(i):
        return pl.ds(pl.multiple_of(i * tr, tr), tr)

    n_slots = wg_buf.shape[0]

    def weight_copies(e, slot):
        copies = []
        for i, (src, dst) in enumerate(((wg_hbm, wg_buf), (wu_hbm, wu_buf), (wd_hbm, wd_buf))):
            rows_c = src.shape[1] // WEIGHT_DMA_CHUNKS
            for c in range(WEIGHT_DMA_CHUNKS):
                rows = pl.ds(c * rows_c, rows_c)
                copies.append(pltpu.make_async_copy(src.at[e, rows], dst.at[slot, rows], wsem.at[slot, i]))
        return copies

    for ahead in range(n_slots - 1):
        for cp in weight_copies(ahead, ahead):
            cp.start()
    load = pltpu.make_async_copy(x_hbm.at[part], xs_ref.at[pl.ds(0, n_rows)], sem.at[0])
    load_dest = pltpu.make_async_copy(dest_hbm.at[part], dest_ref, sem.at[2])
    load_gates = pltpu.make_async_copy(gates_hbm.at[part], gsrc_ref.at[:, pl.ds(0, spare_code)], sem.at[3])
    load.start()
    load_dest.start()
    load_gates.start()
    acc_ref[...] = jnp.zeros_like(acc_ref)
    lhs_ref[...] = jnp.zeros_like(lhs_ref)
    xs_ref[pl.ds(n_rows, tr), :] = jnp.zeros((tr, LANES), F32)
    gsrc_ref[0, spare_code] = 0.0

    def fill_rounding(ex, c):
        first = starts_ref[part * n_e + ex] + counts_ref[part * n_e + ex]
        n_fill = (SUBLANES - first % SUBLANES) % SUBLANES
        for j in range(SUBLANES - 1):
            list_ref[0, jnp.where(j < n_fill, first + j, n_list + j)] = spare_code
        return c

    lax.fori_loop(0, n_e, fill_rounding, 0)
    load_dest.wait()

    def place(t, c):
        for k in range(TOP_K):
            list_ref[0, dest_ref[k, t]] = t * TOP_K + k
        return c

    lax.fori_loop(0, n_tok, place, 0)
    load_gates.wait()
    load.wait()

    def run_block(i0, n_groups, size):
        def gather(g, c):
            base = pl.multiple_of(g * SUBLANES, SUBLANES)
            for r in range(SUBLANES):
                lhs_ref[tile_rows(base + r), :] = xs_ref[code_rows(list_ref[0, i0 + base + r]), :]
            return c

        lax.fori_loop(0, n_groups, gather, 0)
        xb = jnp.concatenate([lhs_ref[pl.ds(s, size, stride=tr), :] for s in range(tr)], axis=-1)
        gu = jnp.dot(xb.astype(BF16), wgu_ref[...], preferred_element_type=F32)
        hid = _silu(gu[:, :D_EXPERT]) * gu[:, D_EXPERT:]
        y = jnp.dot(hid.astype(BF16), wdn_ref[...], preferred_element_type=F32)
        for s in range(tr):
            y_ref[pl.ds(s, size, stride=tr), :] = y[:, s * LANES:(s + 1) * LANES]

        def scatter(g, c):
            base = pl.multiple_of(g * SUBLANES, SUBLANES)
            rows, vals = [], []
            for r in range(SUBLANES):
                code = list_ref[0, i0 + base + r]
                rows.append(code_rows(code))
                vals.append(acc_ref[rows[-1], :] + gsrc_ref[0, code] * y_ref[tile_rows(base + r), :])
            for row, val in zip(rows, vals):
                acc_ref[row, :] = val
            return c

        lax.fori_loop(0, n_groups, scatter, 0)

    def expert(e, c):
        slot = e % n_slots
        ahead = e + n_slots - 1

        @pl.when(ahead < n_e)
        def _():
            for cp in weight_copies(ahead, ahead % n_slots):
                cp.start()

        for cp in weight_copies(e, slot):
            cp.wait()
        start = starts_ref[part * n_e + e]
        count = counts_ref[part * n_e + e]

        @pl.when(count > 0)
        def _():
            wgu_ref[:, :D_EXPERT] = wg_buf[slot].astype(BF16)
            wgu_ref[:, D_EXPERT:] = wu_buf[slot].astype(BF16)
            wdn_ref[...] = wd_buf[slot].astype(BF16)
            n_full = count // MOE_ROWS

            def full_block(bi, cc):
                run_block(start + bi * MOE_ROWS, MOE_ROWS // SUBLANES, MOE_ROWS)
                return cc

            lax.fori_loop(0, n_full, full_block, 0)
            rest = count - n_full * MOE_ROWS
            rest_groups = (rest + SUBLANES - 1) // SUBLANES

            @pl.when(rest > 3 * MOE_ROWS // 4)
            def _():
                run_block(start + n_full * MOE_ROWS, rest_groups, MOE_ROWS)

            @pl.when((rest > MOE_ROWS // 2) & (rest <= 3 * MOE_ROWS // 4))
            def _():
                run_block(start + n_full * MOE_ROWS, rest_groups, 3 * MOE_ROWS // 4)

            @pl.when((rest > 0) & (rest <= MOE_ROWS // 2))
            def _():
                run_block(start + n_full * MOE_ROWS, rest_groups, MOE_ROWS // 2)

        return c

    lax.fori_loop(0, n_e, expert, 0)
    store = pltpu.make_async_copy(acc_ref.at[pl.ds(0, n_rows)], out_hbm.at[part], sem.at[1])
    store.start()
    store.wait()


def _routed_experts(xt, dest, gates, starts, counts, w_gate, w_up, w_down):
    p, n_rows, lanes = xt.shape
    n_e, d, d_e = w_gate.shape
    k, l = dest.shape[1:]
    n_list = l * k + n_e * SUBLANES
    grid_spec = pltpu.PrefetchScalarGridSpec(
        num_scalar_prefetch=2,
        grid=(p,),
        in_specs=[pl.BlockSpec(memory_space=pl.ANY)] * 6,
        out_specs=pl.BlockSpec(memory_space=pl.ANY),
        scratch_shapes=[
            pltpu.VMEM((n_rows + TOKEN_TILE_ROWS, lanes), F32),
            pltpu.VMEM((n_rows + TOKEN_TILE_ROWS, lanes), F32),
            pltpu.VMEM((MOE_ROWS * TOKEN_TILE_ROWS, lanes), F32),
            pltpu.VMEM((MOE_ROWS * TOKEN_TILE_ROWS, lanes), F32),
            pltpu.VMEM((WEIGHT_SLOTS, d, d_e), F32),
            pltpu.VMEM((WEIGHT_SLOTS, d, d_e), F32),
            pltpu.VMEM((WEIGHT_SLOTS, d_e, d), F32),
            pltpu.VMEM((d, 2 * d_e), BF16),
            pltpu.VMEM((d_e, d), BF16),
            pltpu.SMEM((k, l), jnp.int32),
            pltpu.SMEM((1, l * k + LANES), F32),
            pltpu.SMEM((1, n_list + LANES), jnp.int32),
            pltpu.SemaphoreType.DMA((4,)),
            pltpu.SemaphoreType.DMA((WEIGHT_SLOTS, 3)),
        ])
    return pl.pallas_call(
        _moe_kernel,
        out_shape=jax.ShapeDtypeStruct((p, n_rows, lanes), F32),
        grid_spec=grid_spec,
        compiler_params=pltpu.CompilerParams(dimension_semantics=("arbitrary",),
                                             vmem_limit_bytes=VMEM_LIMIT),
        name="routed_experts",
    )(starts, counts, dest, gates, xt, w_gate, w_up, w_down)


def _dispatch_kernel(ids_ref, dest_ref, starts_ref, counts_ref, cnt_ref, *, chunk):
    n_tok = ids_ref.shape[2]
    n_chunks = n_tok // chunk
    eid = lax.broadcasted_iota(jnp.int32, (N_EXPERTS, chunk), 0)

    def one_hots(c):
        cols = pl.ds(pl.multiple_of(c * chunk, chunk), chunk)
        return [jnp.where(eid == ids_ref[0, k:k + 1, cols], 1.0, 0.0) for k in range(TOP_K)], cols

    def count_chunk(c, carry):
        hots, _ = one_hots(c)
        cnt_ref[...] += jnp.sum(sum(hots), axis=1, keepdims=True)
        return carry

    cnt_ref[...] = jnp.zeros_like(cnt_ref)
    lax.fori_loop(0, n_chunks, count_chunk, 0)
    counts = cnt_ref[...]
    counts_ref[0] = counts.astype(jnp.int32)

    rounded = jnp.floor((counts + (SUBLANES - 1)) * (1.0 / SUBLANES)) * SUBLANES
    row = lax.broadcasted_iota(jnp.int32, (N_EXPERTS, LANES), 0)
    scan = jnp.broadcast_to(rounded, (N_EXPERTS, LANES))
    shift = 1
    while shift < N_EXPERTS:
        scan = scan + jnp.where(row >= shift, pltpu.roll(scan, shift, 0), 0.0)
        shift *= 2
    starts = scan[:, 0:1] - rounded
    starts_ref[0] = starts.astype(jnp.int32)

    upper = jnp.where(lax.broadcasted_iota(jnp.int32, (chunk, chunk), 0)
                      < lax.broadcasted_iota(jnp.int32, (chunk, chunk), 1), 1.0, 0.0).astype(BF16)

    def place_chunk(c, seen):
        hots, cols = one_hots(c)
        mask = sum(hots)
        before = jnp.dot(mask.astype(BF16), upper, preferred_element_type=F32)
        pos = starts + seen + before
        for k in range(TOP_K):
            dest_ref[0, k:k + 1, cols] = jnp.sum(hots[k] * pos, axis=0, keepdims=True).astype(jnp.int32)
        return seen + jnp.sum(mask, axis=1, keepdims=True)

    lax.fori_loop(0, n_chunks, place_chunk, jnp.zeros((N_EXPERTS, 1), F32))


def _dispatch(ids):
    p, k, l = ids.shape
    chunk = min(256, l)
    assert l % chunk == 0
    ids_blk = pl.BlockSpec((1, k, l), lambda i: (i, 0, 0))
    col_blk = pl.BlockSpec((1, N_EXPERTS, 1), lambda i: (i, 0, 0))
    dest, starts, counts = pl.pallas_call(
        functools.partial(_dispatch_kernel, chunk=chunk),
        out_shape=(jax.ShapeDtypeStruct((p, k, l), jnp.int32),
                   jax.ShapeDtypeStruct((p, N_EXPERTS, 1), jnp.int32),
                   jax.ShapeDtypeStruct((p, N_EXPERTS, 1), jnp.int32)),
        grid=(p,),
        in_specs=[ids_blk],
        out_specs=(ids_blk, col_blk, col_blk),
        scratch_shapes=[pltpu.VMEM((N_EXPERTS, 1), F32)],
        compiler_params=pltpu.CompilerParams(dimension_semantics=("arbitrary",),
                                             vmem_limit_bytes=VMEM_LIMIT),
        name="dispatch",
    )(ids)
    return dest, starts.reshape(-1), counts.reshape(-1)


def _ffn_out_kernel(x_ref, routed_ref, wgu_ref, wdn_ref, lnvec_ref, o_ref, *, alpha):
    x = x_ref[...]
    rows = x.shape[0]
    routed = jnp.concatenate([routed_ref[pl.ds(s, rows, stride=TOKEN_TILE_ROWS), :]
                              for s in range(TOKEN_TILE_ROWS)], axis=-1)
    gu = jnp.dot(x.astype(BF16), wgu_ref[...], preferred_element_type=F32)
    d_sh = wdn_ref.shape[0]
    hid = _silu(gu[:, :d_sh]) * gu[:, d_sh:]
    shared = jnp.dot(hid.astype(BF16), wdn_ref[...], preferred_element_type=F32)
    o_ref[...] = _layer_norm(alpha * x + routed + shared, lnvec_ref[0:1, :], lnvec_ref[1:2, :])


def _ffn_out(x, routed_tiles, sh_gate, sh_up, sh_down, ln_g, ln_b, alpha):
    n, d = x.shape
    rows = min(FFN_ROWS, n)
    assert n % rows == 0
    wgu = jnp.concatenate([sh_gate, sh_up], axis=1).astype(BF16)
    lnvec = jnp.stack([ln_g, ln_b])
    tile = pl.BlockSpec((rows, d), lambda i: (i, 0))
    tok_tile = pl.BlockSpec((rows * TOKEN_TILE_ROWS, LANES), lambda i: (i, 0))
    return pl.pallas_call(
        functools.partial(_ffn_out_kernel, alpha=alpha),
        out_shape=jax.ShapeDtypeStruct((n, d), F32),
        grid=(n // rows,),
        in_specs=[tile, tok_tile, _full_spec(wgu.shape), _full_spec(sh_down.shape), _full_spec(lnvec.shape)],
        out_specs=tile,
        compiler_params=pltpu.CompilerParams(dimension_semantics=("arbitrary",),
                                             vmem_limit_bytes=VMEM_LIMIT),
        name="shared_ffn_norm",
    )(x, routed_tiles, wgu, sh_down.astype(BF16), lnvec)


def kernel(x, mem, w_in, conv_w, conv_b, conv_ln_g, conv_ln_b, ssm_log_dt, ssm_a_re, ssm_a_im, ssm_b_re, ssm_b_im, ssm_c_re, ssm_c_im, ssm_d, w_glu, b_glu, g_conv_out, g_ssm_out, w_out, ln1_g, ln1_b, wq, wk, wv, wo, ln2_g, ln2_b, w_router, router_bias, w_gate, w_up, w_down, sh_gate, sh_up, sh_down, ln3_g, ln3_b):
    depth = w_in.shape[0]
    alpha = (2 * depth) ** 0.25
    b, l, d = x.shape
    for i in range(depth):
        x_rows = jnp.swapaxes(x, 0, 1).reshape(l * b, d)
        x_rows = _mixer(x_rows, b, alpha, w_in[i], conv_w[i], conv_b[i], conv_ln_g[i], conv_ln_b[i],
                        ssm_log_dt[i], ssm_a_re[i], ssm_a_im[i], ssm_b_re[i], ssm_b_im[i],
                        ssm_c_re[i], ssm_c_im[i], ssm_d[i], w_glu[i], b_glu[i],
                        g_conv_out[i], g_ssm_out[i], w_out[i], ln1_g[i], ln1_b[i])
        x = jnp.swapaxes(x_rows.reshape(l, b, d), 0, 1)
        k, v = _kv_proj(mem, wk[i], wv[i])
        x, xt, ids, gates = _attention_router(x, k, v, wq[i], wo[i], ln2_g[i], ln2_b[i],
                                              w_router[i], router_bias[i], alpha)
        dest, starts, counts = _dispatch(ids)
        gates_tm = jnp.swapaxes(gates, 1, 2).reshape(b, 1, l * TOP_K)
        routed = _routed_experts(xt, dest, gates_tm, starts, counts, w_gate[i], w_up[i], w_down[i])
        x = _ffn_out(x.reshape(b * l, d), routed.reshape(b * l * TOKEN_TILE_ROWS, LANES),
                     sh_gate[i], sh_up[i], sh_down[i], ln3_g[i], ln3_b[i], alpha).reshape(b, l, d)
    return x
```
